```python
import math
import jax
import jax.numpy as jnp
from jax import lax
import numpy as np

D_MODEL = 2048
BATCH = 4
SEQ = 4096
DEPTH = 2

HEAD_DIM = 128
N_HEADS = D_MODEL // HEAD_DIM
N_HEADS_SB = N_HEADS // 2
N_HEADS_MOBA = N_HEADS - N_HEADS_SB
ATTN_WIDTH = N_HEADS * HEAD_DIM
ROPE_DIM = HEAD_DIM // 4
ROPE_THETA = 500000.0
SB_Q_BLOCK = 128
MOBA_BLOCK = 256
MOBA_TOPK = 3
MOBA_Q_CHUNK = 64
LRU_WIDTH = D_MODEL
LRU_BLOCKS = 8
LRU_BLOCK_DIM = LRU_WIDTH // LRU_BLOCKS
LRU_C = 8.0
CONV_WIDTH = 4
D_FF = 7 * D_MODEL // 2
N_EXPERTS = 8
TOP_K = 2
D_EXPERT = 7 * D_MODEL // 2
NORM_EPS = 1e-6
NEG_INF = -1e30
N_EVEN = (DEPTH + 1) // 2
N_ODD = DEPTH // 2

kernel_name = "hybrid_sb_moba_rglru_moe"


def rms_norm(x, g):
    x32 = x.astype(jnp.float32)
    y = x32 * lax.rsqrt(jnp.mean(x32 * x32, axis=-1, keepdims=True) + NORM_EPS)
    return (y * g.astype(jnp.float32)).astype(x.dtype)


def split_heads(t, n_heads):
    b, s, _ = t.shape
    return t.reshape(b, s, n_heads, HEAD_DIM).transpose(0, 2, 1, 3)


def rope_tables(seq_len, dtype):
    inv_freq = ROPE_THETA ** (-jnp.arange(0, ROPE_DIM, 2, dtype=jnp.float32) / ROPE_DIM)
    ang = jnp.arange(seq_len, dtype=jnp.float32)[:, None] * inv_freq[None, :]
    return jnp.cos(ang).astype(dtype), jnp.sin(ang).astype(dtype)


def apply_partial_rope(t, cos, sin):
    half = ROPE_DIM // 2
    t1 = t[..., :half]
    t2 = t[..., half:ROPE_DIM]
    return jnp.concatenate([t1 * cos - t2 * sin, t2 * cos + t1 * sin, t[..., ROPE_DIM:]], axis=-1)


def stick_breaking_attention(q, k, v):
    b, h, s, dh = q.shape
    scale = dh ** -0.5
    kpos = jnp.arange(s)

    def block(i):
        start = i * SB_Q_BLOCK
        qb = lax.dynamic_slice_in_dim(q, start, SB_Q_BLOCK, axis=2)
        qpos = start + jnp.arange(SB_Q_BLOCK)
        past = kpos[None, :] < qpos[:, None]
        z = jnp.einsum('bhqd,bhkd->bhqk', qb, k, preferred_element_type=jnp.float32) * scale
        log_beta = jax.nn.log_sigmoid(z)
        log_keep = jnp.where(past, log_beta - z, 0.0)
        later = lax.cumsum(log_keep, axis=3, reverse=True) - log_keep
        w = jnp.where(past, jnp.exp(log_beta + later), 0.0)
        return jnp.einsum('bhqk,bhkd->bhqd', w.astype(v.dtype), v)

    out = lax.map(block, jnp.arange(s // SB_Q_BLOCK))
    return out.transpose(1, 2, 0, 3, 4).reshape(b, h, s, dh)


def moba_attention(q, k, v):
    b, h, s, dh = q.shape
    scale = dh ** -0.5
    pad = (-s) % MOBA_BLOCK
    if pad:
        k = jnp.pad(k, ((0, 0), (0, 0), (0, pad), (0, 0)))
        v = jnp.pad(v, ((0, 0), (0, 0), (0, pad), (0, 0)))
    n_blocks = (s + pad) // MOBA_BLOCK
    k_sel = min(MOBA_TOPK, n_blocks - 1)
    kb = k.reshape(b, h, n_blocks, MOBA_BLOCK, dh)
    vb = v.reshape(b, h, n_blocks, MOBA_BLOCK, dh)
    k_mean = jnp.mean(kb.astype(jnp.float32), axis=3)
    b_ix = jnp.arange(b)[:, None, None, None]
    h_ix = jnp.arange(h)[None, :, None, None]
    blk_ids = jnp.arange(n_blocks)

    def chunk(c):
        start = c * MOBA_Q_CHUNK
        qc = lax.dynamic_slice_in_dim(q, start, MOBA_Q_CHUNK, axis=2)
        qpos = start + jnp.arange(MOBA_Q_CHUNK)
        own = start // MOBA_BLOCK
        k_own = lax.dynamic_index_in_dim(kb, own, axis=2, keepdims=False)
        v_own = lax.dynamic_index_in_dim(vb, own, axis=2, keepdims=False)
        kpos_own = own * MOBA_BLOCK + jnp.arange(MOBA_BLOCK)
        s_own = jnp.einsum('bhqd,bhkd->bhqk', qc, k_own, preferred_element_type=jnp.float32) * scale
        s_own = jnp.where(kpos_own[None, :] <= qpos[:, None], s_own, NEG_INF)
        if k_sel == 0:
            p_own = jax.nn.softmax(s_own, axis=-1).astype(v.dtype)
            return jnp.einsum('bhqk,bhkd->bhqd', p_own, v_own)
        gate = jnp.einsum('bhqd,bhnd->bhqn', qc.astype(jnp.float32), k_mean)
        gate = jnp.where(blk_ids < own, gate, NEG_INF)
        _, sel = lax.top_k(gate, k_sel)
        valid = jnp.arange(k_sel) < own
        k_g = kb[b_ix, h_ix, sel]
        v_g = vb[b_ix, h_ix, sel]
        s_g = jnp.einsum('bhqd,bhqnkd->bhqnk', qc, k_g, preferred_element_type=jnp.float32) * scale
        s_g = jnp.where(valid[:, None], s_g, NEG_INF).reshape(b, h, MOBA_Q_CHUNK, k_sel * MOBA_BLOCK)
        p = jax.nn.softmax(jnp.concatenate([s_g, s_own], axis=-1), axis=-1).astype(v.dtype)
        p_g = p[..., :k_sel * MOBA_BLOCK].reshape(b, h, MOBA_Q_CHUNK, k_sel, MOBA_BLOCK)
        p_own = p[..., k_sel * MOBA_BLOCK:]
        return (jnp.einsum('bhqnk,bhqnkd->bhqd', p_g, v_g)
                + jnp.einsum('bhqk,bhkd->bhqd', p_own, v_own))

    out = lax.map(chunk, jnp.arange(s // MOBA_Q_CHUNK))
    return out.transpose(1, 2, 0, 3, 4).reshape(b, h, s, dh)


def attention_mixer(x, norm_g, w_in, qn_sb, kn_sb, qn_mb, kn_mb, w_o):
    b, s, _ = x.shape
    hdn = rms_norm(x, norm_g)
    proj = hdn @ w_in
    wa = N_HEADS_SB * HEAD_DIM
    wb = N_HEADS_MOBA * HEAD_DIM
    q_sb, k_sb, v_sb, q_mb, k_mb, v_mb = jnp.split(
        proj, [wa, 2 * wa, 3 * wa, 3 * wa + wb, 3 * wa + 2 * wb], axis=-1)
    q_sb = rms_norm(split_heads(q_sb, N_HEADS_SB), qn_sb)
    k_sb = rms_norm(split_heads(k_sb, N_HEADS_SB), kn_sb)
    v_sb = split_heads(v_sb, N_HEADS_SB)
    cos, sin = rope_tables(s, x.dtype)
    q_mb = apply_partial_rope(rms_norm(split_heads(q_mb, N_HEADS_MOBA), qn_mb), cos, sin)
    k_mb = apply_partial_rope(rms_norm(split_heads(k_mb, N_HEADS_MOBA), kn_mb), cos, sin)
    v_mb = split_heads(v_mb, N_HEADS_MOBA)
    o_sb = stick_breaking_attention(q_sb, k_sb, v_sb)
    o_mb = moba_attention(q_mb, k_mb, v_mb)
    o = jnp.concatenate([o_sb, o_mb], axis=1).transpose(0, 2, 1, 3).reshape(b, s, ATTN_WIDTH)
    return o @ w_o


def rg_lru(x, w_a, b_a, w_i, b_i, lam):
    b, s, _ = x.shape
    xblk = x.reshape(b, s, LRU_BLOCKS, LRU_BLOCK_DIM)
    r = jax.nn.sigmoid(jnp.einsum('bsnd,nde->bsne', xblk, w_a).reshape(b, s, LRU_WIDTH) + b_a)
    i = jax.nn.sigmoid(jnp.einsum('bsnd,nde->bsne', xblk, w_i).reshape(b, s, LRU_WIDTH) + b_i)
    log_a = (-LRU_C * r.astype(jnp.float32)) * jax.nn.softplus(-lam.astype(jnp.float32))
    a = jnp.exp(log_a)
    u = jnp.sqrt(-jnp.expm1(2.0 * log_a)) * (i * x).astype(jnp.float32)

    def combine(left, right):
        a1, b1 = left
        a2, b2 = right
        return a1 * a2, a2 * b1 + b2

    _, hs = lax.associative_scan(combine, (a, u), axis=1)
    return hs.astype(x.dtype)


def recurrent_mixer(x, norm_g, w_in, conv_w, conv_b, w_a, b_a, w_i, b_i, lam, w_o):
    s = x.shape[1]
    hdn = rms_norm(x, norm_g)
    xb, gate = jnp.split(hdn @ w_in, 2, axis=-1)
    xp = jnp.pad(xb, ((0, 0), (CONV_WIDTH - 1, 0), (0, 0)))
    xc = conv_b + xp[:, 0:s] * conv_w[0]
    for tap in range(1, CONV_WIDTH):
        xc = xc + xp[:, tap:tap + s] * conv_w[tap]
    y = rg_lru(xc, w_a, b_a, w_i, b_i, lam)
    return (y * jax.nn.gelu(gate)) @ w_o


def swiglu(h, w_gate, w_up, w_down):
    return (jax.nn.silu(h @ w_gate) * (h @ w_up)) @ w_down


def moe_swiglu(h, router, we_gate, we_up, we_down):
    b, s, d = h.shape
    t = h.reshape(b * s, d)
    logits = jnp.matmul(t, router, preferred_element_type=jnp.float32)
    top_val, top_idx = lax.top_k(logits, TOP_K)
    top_w = jax.nn.softmax(top_val, axis=-1)
    gates = jnp.sum(jax.nn.one_hot(top_idx, N_EXPERTS, dtype=jnp.float32) * top_w[..., None], axis=1)
    out = jnp.zeros_like(t)
    for e in range(N_EXPERTS):
        out = out + gates[:, e:e + 1].astype(t.dtype) * swiglu(t, we_gate[e], we_up[e], we_down[e])
    return out.reshape(b, s, d)


def setup_inputs(seed: int = 0) -> dict:
    key = jax.random.key(seed)
    ks = jax.random.split(key, 32)
    f32 = jnp.float32

    def nrm(k, shape, scale):
        return jax.random.normal(k, shape, f32) * scale

    def gain(k, shape):
        return 1.0 + 0.01 * jax.random.normal(k, shape, f32)

    u = jax.random.uniform(ks[20], (N_ODD, LRU_WIDTH), f32, minval=0.9, maxval=0.999)
    a0 = u ** (1.0 / LRU_C)
    od_lambda = jnp.log(a0) - jnp.log1p(-a0)
    return {
        "x": nrm(ks[0], (BATCH, SEQ, D_MODEL), 1.0),
        "ev_attn_norm": gain(ks[1], (N_EVEN, D_MODEL)),
        "ev_w_in": nrm(ks[2], (N_EVEN, D_MODEL, 3 * ATTN_WIDTH), D_MODEL ** -0.5),
        "ev_q_norm_sb": gain(ks[3], (N_EVEN, HEAD_DIM)),
        "ev_k_norm_sb": gain(ks[4], (N_EVEN, HEAD_DIM)),
        "ev_q_norm_moba": gain(ks[5], (N_EVEN, HEAD_DIM)),
        "ev_k_norm_moba": gain(ks[6], (N_EVEN, HEAD_DIM)),
        "ev_w_o": nrm(ks[7], (N_EVEN, ATTN_WIDTH, D_MODEL), ATTN_WIDTH ** -0.5),
        "ev_ffn_norm": gain(ks[8], (N_EVEN, D_MODEL)),
        "ev_w_gate": nrm(ks[9], (N_EVEN, D_MODEL, D_FF), D_MODEL ** -0.5),
        "ev_w_up": nrm(ks[10], (N_EVEN, D_MODEL, D_FF), D_MODEL ** -0.5),
        "ev_w_down": nrm(ks[11], (N_EVEN, D_FF, D_MODEL), D_FF ** -0.5),
        "od_rec_norm": gain(ks[12], (N_ODD, D_MODEL)),
        "od_w_in": nrm(ks[13], (N_ODD, D_MODEL, 2 * LRU_WIDTH), D_MODEL ** -0.5),
        "od_conv_w": nrm(ks[14], (N_ODD, CONV_WIDTH, LRU_WIDTH), CONV_WIDTH ** -0.5),
        "od_conv_b": nrm(ks[15], (N_ODD, LRU_WIDTH), 0.01),
        "od_w_a": nrm(ks[16], (N_ODD, LRU_BLOCKS, LRU_BLOCK_DIM, LRU_BLOCK_DIM), LRU_BLOCK_DIM ** -0.5),
        "od_b_a": nrm(ks[17], (N_ODD, LRU_WIDTH), 0.01),
        "od_w_i": nrm(ks[18], (N_ODD, LRU_BLOCKS, LRU_BLOCK_DIM, LRU_BLOCK_DIM), LRU_BLOCK_DIM ** -0.5),
        "od_b_i": nrm(ks[19], (N_ODD, LRU_WIDTH), 0.01),
        "od_lambda": od_lambda,
        "od_w_o": nrm(ks[21], (N_ODD, LRU_WIDTH, D_MODEL), LRU_WIDTH ** -0.5),
        "od_ffn_norm": gain(ks[22], (N_ODD, D_MODEL)),
        "od_router": nrm(ks[23], (N_ODD, D_MODEL, N_EXPERTS), D_MODEL ** -0.5),
        "od_we_gate": nrm(ks[24], (N_ODD, N_EXPERTS, D_MODEL, D_EXPERT), D_MODEL ** -0.5),
        "od_we_up": nrm(ks[25], (N_ODD, N_EXPERTS, D_MODEL, D_EXPERT), D_MODEL ** -0.5),
        "od_we_down": nrm(ks[26], (N_ODD, N_EXPERTS, D_EXPERT, D_MODEL), D_EXPERT ** -0.5),
    }


def reference(x, ev_attn_norm, ev_w_in, ev_q_norm_sb, ev_k_norm_sb, ev_q_norm_moba, ev_k_norm_moba,
              ev_w_o, ev_ffn_norm, ev_w_gate, ev_w_up, ev_w_down,
              od_rec_norm, od_w_in, od_conv_w, od_conv_b, od_w_a, od_b_a, od_w_i, od_b_i, od_lambda,
              od_w_o, od_ffn_norm, od_router, od_we_gate, od_we_up, od_we_down):
    for layer in range(DEPTH):
        j = layer // 2
        if layer % 2 == 0:
            x = x + attention_mixer(x, ev_attn_norm[j], ev_w_in[j], ev_q_norm_sb[j], ev_k_norm_sb[j],
                                    ev_q_norm_moba[j], ev_k_norm_moba[j], ev_w_o[j])
            x = x + swiglu(rms_norm(x, ev_ffn_norm[j]), ev_w_gate[j], ev_w_up[j], ev_w_down[j])
        else:
            x = x + recurrent_mixer(x, od_rec_norm[j], od_w_in[j], od_conv_w[j], od_conv_b[j],
                                    od_w_a[j], od_b_a[j], od_w_i[j], od_b_i[j], od_lambda[j], od_w_o[j])
            x = x + moe_swiglu(rms_norm(x, od_ffn_norm[j]), od_router[j],
                               od_we_gate[j], od_we_up[j], od_we_down[j])
    return x
```

```python
import functools
import math

import jax
import jax.numpy as jnp
from jax import lax
from jax.experimental import pallas as pl
from jax.experimental.pallas import tpu as pltpu

F32 = jnp.float32
BF16 = jnp.bfloat16

HEAD_DIM = 128
N_HEADS_SB = 8
N_HEADS_MOBA = 8
ROPE_DIM = HEAD_DIM // 4
ROPE_THETA = 500000.0
MOBA_BLOCK = 256
MOBA_TOPK = 3
LRU_BLOCKS = 8
LRU_C = 8.0
CONV_WIDTH = 4
N_EXPERTS = 8
TOP_K = 2
NORM_EPS = 1e-6
NEG_INF = -1e30

LANES = 128
SUBLANES = 8
VMEM_LIMIT_BYTES = 56 * 1024 * 1024


def _params(semantics, vmem=VMEM_LIMIT_BYTES):
    return pltpu.CompilerParams(dimension_semantics=semantics, vmem_limit_bytes=vmem)


def _rms(x, g):
    ms = jnp.mean(x * x, axis=-1, keepdims=True)
    return x * lax.rsqrt(ms + NORM_EPS) * g


def _norm_matmul_kernel(x_ref, g_ref, w_ref, o_ref, hn_ref):
    @pl.when(pl.program_id(1) == 0)
    def _():
        hn_ref[...] = _rms(x_ref[...], g_ref[...]).astype(BF16)

    y = jnp.dot(hn_ref[...], w_ref[...].astype(BF16), preferred_element_type=F32)
    o_ref[...] = y.astype(o_ref.dtype)


def _qkv_kernel(x_ref, g_ref, w_ref, hg_ref, c_ref, sa_ref, sb_ref, o_ref, hn_ref, *, tn, sec_w):
    j = pl.program_id(1)

    @pl.when(j == 0)
    def _():
        hn_ref[...] = _rms(x_ref[...], g_ref[...]).astype(BF16)

    y = jnp.dot(hn_ref[...], w_ref[...].astype(BF16), preferred_element_type=F32)
    sec = j // (sec_w // tn)
    is_v = jnp.logical_or(sec == 2, sec == 5)

    @pl.when(is_v)
    def _():
        o_ref[...] = y.astype(o_ref.dtype)

    @pl.when(jnp.logical_not(is_v))
    def _():
        gain = hg_ref[0]
        cos = c_ref[0]
        sin_a = sa_ref[0]
        sin_b = sb_ref[0]
        for hh in range(tn // HEAD_DIM):
            t = _rms(y[:, hh * HEAD_DIM:(hh + 1) * HEAD_DIM], gain)
            t = (t * cos + pltpu.roll(t, ROPE_DIM // 2, 1) * sin_a
                 + pltpu.roll(t, HEAD_DIM - ROPE_DIM // 2, 1) * sin_b)
            o_ref[:, hh * HEAD_DIM:(hh + 1) * HEAD_DIM] = t.astype(o_ref.dtype)


def _norm_matmul(x, g, w, *, tm, tn, out_dtype=BF16):
    t, d = x.shape
    n = w.shape[1]
    return pl.pallas_call(
        _norm_matmul_kernel,
        out_shape=jax.ShapeDtypeStruct((t, n), out_dtype),
        grid=(t // tm, n // tn),
        in_specs=[
            pl.BlockSpec((tm, d), lambda i, j: (i, 0)),
            pl.BlockSpec((1, d), lambda i, j: (0, 0)),
            pl.BlockSpec((d, tn), lambda i, j: (0, j)),
        ],
        out_specs=pl.BlockSpec((tm, tn), lambda i, j: (i, j)),
        scratch_shapes=[pltpu.VMEM((tm, d), BF16)],
        compiler_params=_params(("parallel", "arbitrary")),
        name="norm_matmul",
    )(x, g.reshape(1, d), w)


def _rope_tables(seq):
    half = ROPE_DIM // 2
    inv_freq = ROPE_THETA ** (-jnp.arange(0, ROPE_DIM, 2, dtype=F32) / ROPE_DIM)
    ang = jnp.arange(seq, dtype=F32)[:, None] * inv_freq[None, :]
    cos, sin = jnp.cos(ang), jnp.sin(ang)
    ones = jnp.ones((seq, HEAD_DIM - ROPE_DIM), F32)
    zeros = jnp.zeros((seq, HEAD_DIM - ROPE_DIM), F32)
    zh = jnp.zeros((seq, half), F32)
    c = jnp.concatenate([cos, cos, ones], axis=1)
    sa = jnp.concatenate([zh, sin, zeros], axis=1)
    sb = jnp.concatenate([-sin, zh, zeros], axis=1)
    ident = jnp.ones((seq, HEAD_DIM), F32)
    z = jnp.zeros((seq, HEAD_DIM), F32)
    return jnp.stack([ident, c]), jnp.stack([z, sa]), jnp.stack([z, sb])


def _qkv_proj(x, g, w, head_gains, seq, *, tm, tn):
    t, d = x.shape
    n = w.shape[1]
    sec_w = n // 6
    c, sa, sb = _rope_tables(seq)
    n_s = seq // tm
    kern = functools.partial(_qkv_kernel, tn=tn, sec_w=sec_w)
    per_sec = sec_w // tn
    return pl.pallas_call(
        kern,
        out_shape=jax.ShapeDtypeStruct((t, n), BF16),
        grid=(t // tm, n // tn),
        in_specs=[
            pl.BlockSpec((tm, d), lambda i, j: (i, 0)),
            pl.BlockSpec((1, d), lambda i, j: (0, 0)),
            pl.BlockSpec((d, tn), lambda i, j: (0, j)),
            pl.BlockSpec((1, 1, HEAD_DIM), lambda i, j: (j // per_sec, 0, 0)),
            pl.BlockSpec((1, tm, HEAD_DIM), lambda i, j: ((j // per_sec) // 3, i % n_s, 0)),
            pl.BlockSpec((1, tm, HEAD_DIM), lambda i, j: ((j // per_sec) // 3, i % n_s, 0)),
            pl.BlockSpec((1, tm, HEAD_DIM), lambda i, j: ((j // per_sec) // 3, i % n_s, 0)),
        ],
        out_specs=pl.BlockSpec((tm, tn), lambda i, j: (i, j)),
        scratch_shapes=[pltpu.VMEM((tm, d), BF16)],
        compiler_params=_params(("parallel", "arbitrary")),
        name="qkv_proj",
    )(x, g.reshape(1, d), w, head_gains, c, sa, sb)


def _matmul_res_kernel(*refs, n_a):
    a_refs = refs[:n_a]
    w_ref, r_ref, o_ref = refs[n_a:]
    acc = r_ref[...]
    k0 = 0
    for a_ref in a_refs:
        k = a_ref.shape[1]
        acc = acc + jnp.dot(a_ref[...], w_ref[k0:k0 + k, :].astype(BF16), preferred_element_type=F32)
        k0 += k
    o_ref[...] = acc


def _matmul_res(a_list, w, res, *, tm, tn):
    t = res.shape[0]
    k, n = w.shape
    kern = functools.partial(_matmul_res_kernel, n_a=len(a_list))
    a_specs = [pl.BlockSpec((tm, a.shape[1]), lambda i, j: (i, 0)) for a in a_list]
    return pl.pallas_call(
        kern,
        out_shape=jax.ShapeDtypeStruct((t, n), F32),
        grid=(t // tm, n // tn),
        in_specs=a_specs + [
            pl.BlockSpec((k, tn), lambda i, j: (0, j)),
            pl.BlockSpec((tm, tn), lambda i, j: (i, j)),
        ],
        out_specs=pl.BlockSpec((tm, tn), lambda i, j: (i, j)),
        compiler_params=_params(("parallel", "arbitrary")),
        name="matmul_res",
    )(*a_list, w, res)


def _silu_mul(gate, up):
    return gate * (1.0 / (1.0 + jnp.exp(-gate))) * up


def _ffn_kernel(x_ref, g_ref, wg_ref, wu_ref, wd_ref, o_ref, hn_ref):
    @pl.when(pl.program_id(1) == 0)
    def _():
        x = x_ref[...]
        hn_ref[...] = _rms(x, g_ref[...]).astype(BF16)
        o_ref[...] = x

    hn = hn_ref[...]
    gate = jnp.dot(hn, wg_ref[...].astype(BF16), preferred_element_type=F32)
    up = jnp.dot(hn, wu_ref[...].astype(BF16), preferred_element_type=F32)
    h = _silu_mul(gate, up).astype(BF16)
    o_ref[...] += jnp.dot(h, wd_ref[...].astype(BF16), preferred_element_type=F32)


def _ffn(x, g, wg, wu, wd, *, tm, tf):
    t, d = x.shape
    dff = wg.shape[1]
    return pl.pallas_call(
        _ffn_kernel,
        out_shape=jax.ShapeDtypeStruct((t, d), F32),
        grid=(t // tm, dff // tf),
        in_specs=[
            pl.BlockSpec((tm, d), lambda i, f: (i, 0), pipeline_mode=pl.Buffered(1)),
            pl.BlockSpec((1, d), lambda i, f: (0, 0)),
            pl.BlockSpec((d, tf), lambda i, f: (0, f)),
            pl.BlockSpec((d, tf), lambda i, f: (0, f)),
            pl.BlockSpec((tf, d), lambda i, f: (f, 0)),
        ],
        out_specs=pl.BlockSpec((tm, d), lambda i, f: (i, 0)),
        scratch_shapes=[pltpu.VMEM((tm, d), BF16)],
        compiler_params=_params(("parallel", "arbitrary")),
        name="ffn_swiglu",
    )(x, g.reshape(1, d), wg, wu, wd)


def _sb_kernel(q_ref, k_ref, v_ref, o_ref, acc_ref, c_ref, *, bq, bk):
    qi = pl.program_id(2)
    q = q_ref[...]
    acc_ref[...] = jnp.zeros_like(acc_ref)
    c_ref[...] = jnp.zeros_like(c_ref)
    row = lax.broadcasted_iota(jnp.int32, (bq, bk), 0)
    col = lax.broadcasted_iota(jnp.int32, (bq, bk), 1)
    kk = lax.broadcasted_iota(jnp.int32, (bk, bk), 0)
    ks = lax.broadcasted_iota(jnp.int32, (bk, bk), 1)
    upper = jnp.where(kk > ks, 1.0, 0.0).astype(BF16)

    def body(it, carry):
        kj = qi - it
        start = pl.multiple_of(kj * bk, bk)
        kb = k_ref[pl.ds(start, bk), :]
        vb = v_ref[pl.ds(start, bk), :]
        z = lax.dot_general(q, kb, (((1,), (1,)), ((), ())), preferred_element_type=F32)
        sp = jnp.log1p(jnp.exp(-jnp.abs(z)))
        log_beta = jnp.minimum(z, 0.0) - sp
        past = (kj * bk + col) < (qi * bq + row)
        log_keep = jnp.where(past, log_beta - z, 0.0)
        lk_hi = log_keep.astype(BF16)
        lk_lo = (log_keep - lk_hi.astype(F32)).astype(BF16)
        later = (jnp.dot(lk_hi, upper, preferred_element_type=F32)
                 + jnp.dot(lk_lo, upper, preferred_element_type=F32))
        w = jnp.where(past, jnp.exp(log_beta + later + c_ref[...]), 0.0)
        acc_ref[...] += jnp.dot(w.astype(BF16), vb, preferred_element_type=F32)
        c_ref[...] += jnp.sum(log_keep, axis=1, keepdims=True)
        return carry

    lax.fori_loop(0, qi + 1, body, 0)
    o_ref[...] = acc_ref[...].astype(o_ref.dtype)


def _sb_attention(qkv, batch, seq, *, bq, bk):
    t = qkv.shape[0]
    nq = seq // bq
    kern = functools.partial(_sb_kernel, bq=bq, bk=bk)
    return pl.pallas_call(
        kern,
        out_shape=jax.ShapeDtypeStruct((t, N_HEADS_SB * HEAD_DIM), BF16),
        grid=(batch, N_HEADS_SB, nq),
        in_specs=[
            pl.BlockSpec((bq, HEAD_DIM), lambda b, h, i: (b * nq + i, h)),
            pl.BlockSpec((seq, HEAD_DIM), lambda b, h, i: (b, N_HEADS_SB + h)),
            pl.BlockSpec((seq, HEAD_DIM), lambda b, h, i: (b, 2 * N_HEADS_SB + h)),
        ],
        out_specs=pl.BlockSpec((bq, HEAD_DIM), lambda b, h, i: (b * nq + i, h)),
        scratch_shapes=[pltpu.VMEM((bq, HEAD_DIM), F32), pltpu.VMEM((bq, 1), F32)],
        compiler_params=_params(("parallel", "parallel", "arbitrary")),
        name="sb_attention",
    )(qkv, qkv, qkv)


def _moba_kernel(q_ref, k_ref, v_ref, o_ref, kmean_ref, acc_ref, m_ref, l_ref, *, blk, seq):
    qi = pl.program_id(2)
    n_blocks = seq // blk

    @pl.when(qi == 0)
    def _():
        r = lax.broadcasted_iota(jnp.int32, (LANES, seq), 0)
        s = lax.broadcasted_iota(jnp.int32, (LANES, seq), 1)
        avg = jnp.where(s // blk == r, 1.0 / blk, 0.0).astype(BF16)
        kmean_ref[...] = jnp.dot(avg, k_ref[...], preferred_element_type=F32)

    q = q_ref[...]
    gate = lax.dot_general(q.astype(F32), kmean_ref[...], (((1,), (1,)), ((), ())),
                           precision=lax.Precision.HIGHEST, preferred_element_type=F32)
    lane = lax.broadcasted_iota(jnp.int32, (blk, LANES), 1)
    lowest = jnp.float32(-3.0e38)
    g = jnp.where(lane < qi, gate, lowest)
    bias = jnp.full((blk, LANES), NEG_INF, F32)
    for _ in range(MOBA_TOPK):
        mx = jnp.max(g, axis=1, keepdims=True)
        first = jnp.min(jnp.where(g == mx, lane, LANES), axis=1, keepdims=True)
        pick = lane == first
        bias = jnp.where(jnp.logical_and(pick, lane < qi), 0.0, bias)
        g = jnp.where(pick, lowest, g)

    start = pl.multiple_of(qi * blk, blk)
    kb = k_ref[pl.ds(start, blk), :]
    vb = v_ref[pl.ds(start, blk), :]
    s_own = lax.dot_general(q, kb, (((1,), (1,)), ((), ())), preferred_element_type=F32)
    row = lax.broadcasted_iota(jnp.int32, (blk, blk), 0)
    col = lax.broadcasted_iota(jnp.int32, (blk, blk), 1)
    s_own = jnp.where(col <= row, s_own, NEG_INF)
    m0 = jnp.max(s_own, axis=1, keepdims=True)
    p0 = jnp.exp(s_own - m0)
    m_ref[...] = m0
    l_ref[...] = jnp.sum(p0, axis=1, keepdims=True)
    acc_ref[...] = jnp.dot(p0.astype(BF16), vb, preferred_element_type=F32)

    def body(j, carry):
        st = pl.multiple_of(j * blk, blk)
        kj = k_ref[pl.ds(st, blk), :]
        vj = v_ref[pl.ds(st, blk), :]
        bias_j = jnp.sum(jnp.where(lane == j, bias, 0.0), axis=1, keepdims=True)
        s = lax.dot_general(q, kj, (((1,), (1,)), ((), ())), preferred_element_type=F32) + bias_j
        m_old = m_ref[...]
        m_new = jnp.maximum(m_old, jnp.max(s, axis=1, keepdims=True))
        alpha = jnp.exp(m_old - m_new)
        p = jnp.exp(s - m_new)
        l_ref[...] = alpha * l_ref[...] + jnp.sum(p, axis=1, keepdims=True)
        acc_ref[...] = alpha * acc_ref[...] + jnp.dot(p.astype(BF16), vj, preferred_element_type=F32)
        m_ref[...] = m_new
        return carry

    lax.fori_loop(0, qi, body, 0)
    o_ref[...] = (acc_ref[...] / l_ref[...]).astype(o_ref.dtype)


def _moba_attention(qkv, batch, seq):
    t = qkv.shape[0]
    blk = MOBA_BLOCK
    nq = seq // blk
    col0 = 3 * N_HEADS_SB
    kern = functools.partial(_moba_kernel, blk=blk, seq=seq)
    return pl.pallas_call(
        kern,
        out_shape=jax.ShapeDtypeStruct((t, N_HEADS_MOBA * HEAD_DIM), BF16),
        grid=(batch, N_HEADS_MOBA, nq),
        in_specs=[
            pl.BlockSpec((blk, HEAD_DIM), lambda b, h, i: (b * nq + i, col0 + h)),
            pl.BlockSpec((seq, HEAD_DIM), lambda b, h, i: (b, col0 + N_HEADS_MOBA + h)),
            pl.BlockSpec((seq, HEAD_DIM), lambda b, h, i: (b, col0 + 2 * N_HEADS_MOBA + h)),
        ],
        out_specs=pl.BlockSpec((blk, HEAD_DIM), lambda b, h, i: (b * nq + i, h)),
        scratch_shapes=[
            pltpu.VMEM((LANES, HEAD_DIM), F32),
            pltpu.VMEM((blk, HEAD_DIM), F32),
            pltpu.VMEM((blk, 1), F32),
            pltpu.VMEM((blk, 1), F32),
        ],
        compiler_params=_params(("parallel", "parallel", "arbitrary")),
        name="moba_attention",
    )(qkv, qkv, qkv)


def _attention_layer(x, seq, batch, attn_norm, w_in, qn_sb, kn_sb, qn_mb, kn_mb, w_o,
                     ffn_norm, w_gate, w_up, w_down, *, tm):
    scale = HEAD_DIM ** -0.5
    ones = jnp.ones((HEAD_DIM,), F32)
    head_gains = jnp.stack([qn_sb * scale, kn_sb, ones, qn_mb * scale, kn_mb, ones]).reshape(6, 1, HEAD_DIM)
    qkv = _qkv_proj(x, attn_norm, w_in, head_gains, seq, tm=tm, tn=512)
    o_sb = _sb_attention(qkv, batch, seq, bq=256, bk=256)
    o_mb = _moba_attention(qkv, batch, seq)
    x = _matmul_res([o_sb, o_mb], w_o, x, tm=tm, tn=512)
    return _ffn(x, ffn_norm, w_gate, w_up, w_down, tm=tm, tf=256)


def _sigmoid(x):
    return 1.0 / (1.0 + jnp.exp(-x))


def _gelu_tanh(x):
    return 0.5 * x * (1.0 + jnp.tanh(math.sqrt(2.0 / math.pi) * (x + 0.044715 * (x * x * x))))


def _rglru_kernel(xb_ref, gt_ref, cw_ref, cb_ref, wa_ref, ba_ref, wi_ref, bi_ref, lam_ref, o_ref,
                  xpad_ref, h_ref, a_ref, b_ref, *, tt, pad):
    ti = pl.program_id(2)

    @pl.when(ti == 0)
    def _():
        xpad_ref[0:SUBLANES, :] = jnp.zeros((SUBLANES, xpad_ref.shape[1]), F32)
        h_ref[...] = jnp.zeros_like(h_ref)
        a_ref[0:pad, :] = jnp.ones((pad, a_ref.shape[1]), F32)
        b_ref[0:pad, :] = jnp.zeros((pad, b_ref.shape[1]), F32)

    xb = xb_ref[...].astype(F32)
    xpad_ref[SUBLANES:SUBLANES + tt, :] = xb
    cw = cw_ref[...]
    xc = cb_ref[...] + xb * cw[CONV_WIDTH - 1:CONV_WIDTH, :]
    for tap in range(CONV_WIDTH - 1):
        off = SUBLANES - (CONV_WIDTH - 1) + tap
        xc = xc + xpad_ref[off:off + tt, :] * cw[tap:tap + 1, :]
    xpad_ref[0:SUBLANES, :] = xpad_ref[tt:tt + SUBLANES, :]

    xcb = xc.astype(BF16)
    r = _sigmoid(jnp.dot(xcb, wa_ref[0].astype(BF16), preferred_element_type=F32) + ba_ref[...])
    ig = _sigmoid(jnp.dot(xcb, wi_ref[0].astype(BF16), preferred_element_type=F32) + bi_ref[...])
    neg_lam = -lam_ref[...]
    sp = jnp.maximum(neg_lam, 0.0) + jnp.log1p(jnp.exp(-jnp.abs(neg_lam)))
    log_a = (-LRU_C * r) * sp
    a = jnp.exp(log_a)
    u = jnp.sqrt(-jnp.tanh(log_a) * (a * a + 1.0)) * (ig * xc)

    a_ref[pad:pad + tt, :] = a
    b_ref[pad:pad + tt, :] = u
    s = 1
    while s < tt:
        a_cur = a_ref[pad:pad + tt, :]
        b_cur = b_ref[pad:pad + tt, :]
        a_sh = a_ref[pad - s:pad - s + tt, :]
        b_sh = b_ref[pad - s:pad - s + tt, :]
        b_ref[pad:pad + tt, :] = a_cur * b_sh + b_cur
        a_ref[pad:pad + tt, :] = a_cur * a_sh
        s *= 2
    h = a_ref[pad:pad + tt, :] * h_ref[0:1, :] + b_ref[pad:pad + tt, :]
    h_ref[0:1, :] = h[tt - 1:tt, :]
    o_ref[...] = (h * _gelu_tanh(gt_ref[...].astype(F32))).astype(o_ref.dtype)


def _rglru(xbg, batch, seq, conv_w, conv_b, w_a, b_a, w_i, b_i, lam, *, tt):
    t = xbg.shape[0]
    width = xbg.shape[1] // 2
    bw = width // LRU_BLOCKS
    n_t = seq // tt
    pad = tt // 2
    kern = functools.partial(_rglru_kernel, tt=tt, pad=pad)
    vec = lambda a: a.reshape(1, width)
    vspec = pl.BlockSpec((1, bw), lambda b, c, i: (0, c))
    return pl.pallas_call(
        kern,
        out_shape=jax.ShapeDtypeStruct((t, width), BF16),
        grid=(batch, LRU_BLOCKS, n_t),
        in_specs=[
            pl.BlockSpec((tt, bw), lambda b, c, i: (b * n_t + i, c)),
            pl.BlockSpec((tt, bw), lambda b, c, i: (b * n_t + i, LRU_BLOCKS + c)),
            pl.BlockSpec((CONV_WIDTH, bw), lambda b, c, i: (0, c)),
            vspec,
            pl.BlockSpec((1, bw, bw), lambda b, c, i: (c, 0, 0)),
            vspec,
            pl.BlockSpec((1, bw, bw), lambda b, c, i: (c, 0, 0)),
            vspec,
            vspec,
        ],
        out_specs=pl.BlockSpec((tt, bw), lambda b, c, i: (b * n_t + i, c)),
        scratch_shapes=[
            pltpu.VMEM((tt + 2 * SUBLANES, bw), F32),
            pltpu.VMEM((SUBLANES, bw), F32),
            pltpu.VMEM((pad + tt, bw), F32),
            pltpu.VMEM((pad + tt, bw), F32),
        ],
        compiler_params=_params(("parallel", "parallel", "arbitrary")),
        name="rglru",
    )(xbg, xbg, conv_w, vec(conv_b), w_a, vec(b_a), w_i, vec(b_i), vec(lam))


def _recurrent_block(x, seq, batch, rec_norm, w_in, conv_w, conv_b, w_a, b_a, w_i, b_i, lam, w_o, *, tm):
    xbg = _norm_matmul(x, rec_norm, w_in, tm=tm, tn=512)
    y = _rglru(xbg, batch, seq, conv_w, conv_b, w_a, b_a, w_i, b_i, lam, tt=min(512, seq))
    return _matmul_res([y], w_o, x, tm=tm, tn=512)


def _router_kernel(x_ref, g_ref, wr_ref, idx_ref, wt_ref, cnt_ref, run_ref, *, rows):
    @pl.when(pl.program_id(0) == 0)
    def _():
        run_ref[...] = jnp.zeros_like(run_ref)

    hn = _rms(x_ref[...], g_ref[...])
    logits = jnp.dot(hn, wr_ref[...], precision=lax.Precision.HIGHEST, preferred_element_type=F32)
    lane = lax.broadcasted_iota(jnp.int32, (rows, LANES), 1)
    lowest = jnp.float32(-3.0e38)
    logits = jnp.where(lane < N_EXPERTS, logits, lowest)
    m1 = jnp.max(logits, axis=1, keepdims=True)
    e1 = jnp.min(jnp.where(logits == m1, lane, LANES), axis=1, keepdims=True)
    rest = jnp.where(lane == e1, lowest, logits)
    m2 = jnp.max(rest, axis=1, keepdims=True)
    e2 = jnp.min(jnp.where(rest == m2, lane, LANES), axis=1, keepdims=True)
    ex = jnp.exp(m2 - m1)
    w1 = 1.0 / (1.0 + ex)
    w2 = ex / (1.0 + ex)

    onehot = jnp.where(jnp.logical_or(lane == e1, lane == e2), 1.0, 0.0)
    rr = lax.broadcasted_iota(jnp.int32, (rows, rows), 0)
    cc = lax.broadcasted_iota(jnp.int32, (rows, rows), 1)
    before = jnp.where(cc < rr, 1.0, 0.0).astype(BF16)
    rank = jnp.dot(before, onehot.astype(BF16), preferred_element_type=F32) + run_ref[0:1, :]
    r1 = jnp.sum(jnp.where(lane == e1, rank, 0.0), axis=1, keepdims=True).astype(jnp.int32)
    r2 = jnp.sum(jnp.where(lane == e2, rank, 0.0), axis=1, keepdims=True).astype(jnp.int32)
    run_ref[0:1, :] = run_ref[0:1, :] + jnp.sum(onehot, axis=0, keepdims=True)

    packed = jnp.where(lane == 0, e1, jnp.where(lane == 1, e2, jnp.where(lane == 2, r1, r2)))
    idx_ref[...] = packed[:, 0:idx_ref.shape[1]]
    wts = jnp.where(lane == 0, w1, w2)
    wt_ref[...] = wts[:, 0:wt_ref.shape[1]]
    cnt_ref[...] = run_ref[...].astype(jnp.int32)


def _router(x, g, router, *, rows):
    t, d = x.shape
    wr = jnp.pad(router, ((0, 0), (0, LANES - router.shape[1])))
    kern = functools.partial(_router_kernel, rows=rows)
    return pl.pallas_call(
        kern,
        out_shape=(
            jax.ShapeDtypeStruct((t, SUBLANES), jnp.int32),
            jax.ShapeDtypeStruct((t, SUBLANES), F32),
            jax.ShapeDtypeStruct((SUBLANES, LANES), jnp.int32),
        ),
        grid=(t // rows,),
        in_specs=[
            pl.BlockSpec((rows, d), lambda i: (i, 0)),
            pl.BlockSpec((1, d), lambda i: (0, 0)),
            pl.BlockSpec((d, LANES), lambda i: (0, 0)),
        ],
        out_specs=(
            pl.BlockSpec((rows, SUBLANES), lambda i: (i, 0)),
            pl.BlockSpec((rows, SUBLANES), lambda i: (i, 0)),
            pl.BlockSpec((SUBLANES, LANES), lambda i: (0, 0)),
        ),
        scratch_shapes=[pltpu.VMEM((SUBLANES, LANES), F32)],
        compiler_params=_params(("arbitrary",)),
        name="moe_router",
    )(x, g.reshape(1, d), wr)


DISPATCH_ROWS = 512


def _dispatch_kernel(pos_ref, x_ref, xs_in_ref, xs_ref, sem):
    del xs_in_ref
    base = pl.program_id(0) * DISPATCH_ROWS

    def row_copy(r, k):
        return pltpu.make_async_copy(x_ref.at[pl.ds(base + r, 1)], xs_ref.at[pl.ds(pos_ref[2 * r + k], 1)], sem)

    def start(r, carry):
        row_copy(r, 0).start()
        row_copy(r, 1).start()
        return carry

    def wait(r, carry):
        row_copy(r, 0).wait()
        row_copy(r, 1).wait()
        return carry

    lax.fori_loop(0, DISPATCH_ROWS, start, 0)
    lax.fori_loop(0, DISPATCH_ROWS, wait, 0)


def _dispatch(x, pos_flat, n_rows):
    t, d = x.shape
    xs0 = jnp.zeros((n_rows, d), F32)
    return pl.pallas_call(
        _dispatch_kernel,
        out_shape=jax.ShapeDtypeStruct((n_rows, d), F32),
        grid=(t // DISPATCH_ROWS,),
        in_specs=[
            pl.BlockSpec((2 * DISPATCH_ROWS,), lambda i: (i,), memory_space=pltpu.SMEM),
            pl.BlockSpec(memory_space=pl.ANY),
            pl.BlockSpec(memory_space=pl.ANY),
        ],
        out_specs=pl.BlockSpec(memory_space=pl.ANY),
        scratch_shapes=[pltpu.SemaphoreType.DMA(())],
        input_output_aliases={2: 0},
        compiler_params=pltpu.CompilerParams(dimension_semantics=("arbitrary",), has_side_effects=True),
        name="moe_dispatch",
    )(pos_flat, x, xs0)


def _moe_ffn_kernel(te_ref, nu_ref, xs_ref, g_ref, wg_ref, wu_ref, wd_ref, o_ref, hn_ref):
    del te_ref
    i = pl.program_id(0)
    first = pl.program_id(1) == 0

    @pl.when(jnp.logical_and(i >= nu_ref[0], first))
    def _():
        o_ref[...] = jnp.zeros_like(o_ref)

    @pl.when(i < nu_ref[0])
    def _():
        @pl.when(first)
        def _():
            hn_ref[...] = _rms(xs_ref[...], g_ref[...]).astype(BF16)
            o_ref[...] = jnp.zeros_like(o_ref)

        hn = hn_ref[...]
        gate = jnp.dot(hn, wg_ref[0].astype(BF16), preferred_element_type=F32)
        up = jnp.dot(hn, wu_ref[0].astype(BF16), preferred_element_type=F32)
        h = _silu_mul(gate, up).astype(BF16)
        o_ref[...] += jnp.dot(h, wd_ref[0].astype(BF16), preferred_element_type=F32)


def _moe_ffn(xs, g, we_gate, we_up, we_down, tile_expert, n_used, *, tm, tf):
    n_rows, d = xs.shape
    dff = we_gate.shape[2]
    n_f = dff // tf
    n_tiles = n_rows // tm

    def tile(i, nu):
        return jnp.minimum(i, nu[0] - 1)

    def fidx(i, f, nu):
        return jnp.where(i < nu[0], f, n_f - 1)

    grid_spec = pltpu.PrefetchScalarGridSpec(
        num_scalar_prefetch=2,
        grid=(n_tiles, n_f),
        in_specs=[
            pl.BlockSpec((tm, d), lambda i, f, te, nu: (tile(i, nu), 0), pipeline_mode=pl.Buffered(1)),
            pl.BlockSpec((1, d), lambda i, f, te, nu: (0, 0)),
            pl.BlockSpec((1, d, tf), lambda i, f, te, nu: (te[tile(i, nu)], 0, fidx(i, f, nu))),
            pl.BlockSpec((1, d, tf), lambda i, f, te, nu: (te[tile(i, nu)], 0, fidx(i, f, nu))),
            pl.BlockSpec((1, tf, d), lambda i, f, te, nu: (te[tile(i, nu)], fidx(i, f, nu), 0)),
        ],
        out_specs=pl.BlockSpec((tm, d), lambda i, f, te, nu: (i, 0)),
        scratch_shapes=[pltpu.VMEM((tm, d), BF16)],
    )
    return pl.pallas_call(
        _moe_ffn_kernel,
        out_shape=jax.ShapeDtypeStruct((n_rows, d), F32),
        grid_spec=grid_spec,
        compiler_params=_params(("arbitrary", "arbitrary")),
        name="moe_ffn",
    )(tile_expert, n_used, xs, g.reshape(1, d), we_gate, we_up, we_down)


COMBINE_ROWS = 512


def _combine_kernel(pos_ref, x_ref, wt_ref, ys_ref, o_ref, y1_ref, y2_ref, sem):
    def row_copy(r, k, dst):
        return pltpu.make_async_copy(ys_ref.at[pl.ds(pos_ref[2 * r + k], 1)], dst.at[pl.ds(r, 1)], sem)

    def start(r, carry):
        row_copy(r, 0, y1_ref).start()
        row_copy(r, 1, y2_ref).start()
        return carry

    def wait(r, carry):
        row_copy(r, 0, y1_ref).wait()
        row_copy(r, 1, y2_ref).wait()
        return carry

    lax.fori_loop(0, COMBINE_ROWS, start, 0)
    lax.fori_loop(0, COMBINE_ROWS, wait, 0)
    wt = wt_ref[...]
    o_ref[...] = x_ref[...] + wt[:, 0:1] * y1_ref[...] + wt[:, 1:2] * y2_ref[...]


def _combine(x, wts, ys, pos_flat):
    t, d = x.shape
    return pl.pallas_call(
        _combine_kernel,
        out_shape=jax.ShapeDtypeStruct((t, d), F32),
        grid=(t // COMBINE_ROWS,),
        in_specs=[
            pl.BlockSpec((2 * COMBINE_ROWS,), lambda i: (i,), memory_space=pltpu.SMEM),
            pl.BlockSpec((COMBINE_ROWS, d), lambda i: (i, 0)),
            pl.BlockSpec((COMBINE_ROWS, SUBLANES), lambda i: (i, 0)),
            pl.BlockSpec(memory_space=pl.ANY),
        ],
        out_specs=pl.BlockSpec((COMBINE_ROWS, d), lambda i: (i, 0)),
        scratch_shapes=[
            pltpu.VMEM((COMBINE_ROWS, d), F32),
            pltpu.VMEM((COMBINE_ROWS, d), F32),
            pltpu.SemaphoreType.DMA(()),
        ],
        compiler_params=_params(("arbitrary",)),
        name="moe_combine",
    )(pos_flat, x, wts, ys)


def _moe_block(x, ffn_norm, router, we_gate, we_up, we_down, *, tm=1024, tf=256):
    t, d = x.shape
    idx, wts, cnt = _router(x, ffn_norm, router, rows=256)
    n_tiles = (TOP_K * t) // tm + N_EXPERTS
    counts = cnt[0, :N_EXPERTS]
    padded = ((counts + tm - 1) // tm) * tm
    ends = jnp.cumsum(padded)
    offsets = ends - padded
    pos = jnp.take(offsets, idx[:, 0:2]) + idx[:, 2:4]
    pos_flat = pos.reshape(-1).astype(jnp.int32)
    tile_start = jnp.arange(n_tiles, dtype=jnp.int32) * tm
    tile_expert = jnp.minimum(jnp.sum(tile_start[:, None] >= ends[None, :], axis=1), N_EXPERTS - 1).astype(jnp.int32)
    n_used = (ends[-1:] // tm).astype(jnp.int32)
    xs = _dispatch(x, pos_flat, n_tiles * tm)
    ys = _moe_ffn(xs, ffn_norm, we_gate, we_up, we_down, tile_expert, n_used, tm=tm, tf=tf)
    return _combine(x, wts, ys, pos_flat)


def kernel(x, ev_attn_norm, ev_w_in, ev_q_norm_sb, ev_k_norm_sb, ev_q_norm_moba, ev_k_norm_moba, ev_w_o, ev_ffn_norm, ev_w_gate, ev_w_up, ev_w_down, od_rec_norm, od_w_in, od_conv_w, od_conv_b, od_w_a, od_b_a, od_w_i, od_b_i, od_lambda, od_w_o, od_ffn_norm, od_router, od_we_gate, od_we_up, od_we_down):
    batch, seq, d = x.shape
    t = batch * seq
    tm = min(1024, seq)
    h = x.reshape(t, d)
    h = _attention_layer(h, seq, batch, ev_attn_norm[0], ev_w_in[0], ev_q_norm_sb[0], ev_k_norm_sb[0],
                         ev_q_norm_moba[0], ev_k_norm_moba[0], ev_w_o[0], ev_ffn_norm[0],
                         ev_w_gate[0], ev_w_up[0], ev_w_down[0], tm=tm)
    h = _recurrent_block(h, seq, batch, od_rec_norm[0], od_w_in[0], od_conv_w[0], od_conv_b[0], od_w_a[0],
                         od_b_a[0], od_w_i[0], od_b_i[0], od_lambda[0], od_w_o[0], tm=tm)
    h = _moe_block(h, od_ffn_norm[0], od_router[0], od_we_gate[0], od_we_up[0], od_we_down[0])
    return h.reshape(batch, seq, d)
```

```python
import functools
import math

import jax
import jax.numpy as jnp
from jax import lax
from jax.experimental import pallas as pl
from jax.experimental.pallas import tpu as pltpu

F32 = jnp.float32
BF16 = jnp.bfloat16

HEAD_DIM = 128
N_HEADS_SB = 8
N_HEADS_MOBA = 8
ROPE_DIM = HEAD_DIM // 4
ROPE_THETA = 500000.0
MOBA_BLOCK = 256
MOBA_TOPK = 3
LRU_BLOCKS = 8
LRU_C = 8.0
CONV_WIDTH = 4
N_EXPERTS = 8
TOP_K = 2
NORM_EPS = 1e-6
NEG_INF = -1e30

LANES = 128
SUBLANES = 8
VMEM_LIMIT_BYTES = 56 * 1024 * 1024


def _params(semantics, vmem=VMEM_LIMIT_BYTES):
    return pltpu.CompilerParams(dimension_semantics=semantics, vmem_limit_bytes=vmem)


def _rms(x, g):
    ms = jnp.mean(x * x, axis=-1, keepdims=True)
    return x * lax.rsqrt(ms + NORM_EPS) * g


def _norm_matmul_kernel(x_ref, g_ref, w_ref, o_ref, hn_ref):
    @pl.when(pl.program_id(1) == 0)
    def _():
        hn_ref[...] = _rms(x_ref[...], g_ref[...]).astype(BF16)

    y = jnp.dot(hn_ref[...], w_ref[...].astype(BF16), preferred_element_type=F32)
    o_ref[...] = y.astype(o_ref.dtype)


def _qkv_kernel(x_ref, g_ref, w_ref, hg_ref, c_ref, sa_ref, sb_ref, o_ref, hn_ref, *, tn, sec_w):
    j = pl.program_id(1)

    @pl.when(j == 0)
    def _():
        hn_ref[...] = _rms(x_ref[...], g_ref[...]).astype(BF16)

    y = jnp.dot(hn_ref[...], w_ref[...].astype(BF16), preferred_element_type=F32)
    sec = j // (sec_w // tn)
    is_v = jnp.logical_or(sec == 2, sec == 5)

    @pl.when(is_v)
    def _():
        o_ref[...] = y.astype(o_ref.dtype)

    @pl.when(jnp.logical_not(is_v))
    def _():
        gain = hg_ref[0]
        cos = c_ref[0]
        sin_a = sa_ref[0]
        sin_b = sb_ref[0]
        for hh in range(tn // HEAD_DIM):
            t = _rms(y[:, hh * HEAD_DIM:(hh + 1) * HEAD_DIM], gain)
            t = (t * cos + pltpu.roll(t, ROPE_DIM // 2, 1) * sin_a
                 + pltpu.roll(t, HEAD_DIM - ROPE_DIM // 2, 1) * sin_b)
            o_ref[:, hh * HEAD_DIM:(hh + 1) * HEAD_DIM] = t.astype(o_ref.dtype)


def _norm_matmul(x, g, w, *, tm, tn, out_dtype=BF16):
    t, d = x.shape
    n = w.shape[1]
    return pl.pallas_call(
        _norm_matmul_kernel,
        out_shape=jax.ShapeDtypeStruct((t, n), out_dtype),
        grid=(t // tm, n // tn),
        in_specs=[
            pl.BlockSpec((tm, d), lambda i, j: (i, 0)),
            pl.BlockSpec((1, d), lambda i, j: (0, 0)),
            pl.BlockSpec((d, tn), lambda i, j: (0, j)),
        ],
        out_specs=pl.BlockSpec((tm, tn), lambda i, j: (i, j)),
        scratch_shapes=[pltpu.VMEM((tm, d), BF16)],
        compiler_params=_params(("parallel", "arbitrary")),
        name="norm_matmul",
    )(x, g.reshape(1, d), w)


def _rope_tables(seq):
    half = ROPE_DIM // 2
    inv_freq = ROPE_THETA ** (-jnp.arange(0, ROPE_DIM, 2, dtype=F32) / ROPE_DIM)
    ang = jnp.arange(seq, dtype=F32)[:, None] * inv_freq[None, :]
    cos, sin = jnp.cos(ang), jnp.sin(ang)
    ones = jnp.ones((seq, HEAD_DIM - ROPE_DIM), F32)
    zeros = jnp.zeros((seq, HEAD_DIM - ROPE_DIM), F32)
    zh = jnp.zeros((seq, half), F32)
    c = jnp.concatenate([cos, cos, ones], axis=1)
    sa = jnp.concatenate([zh, sin, zeros], axis=1)
    sb = jnp.concatenate([-sin, zh, zeros], axis=1)
    ident = jnp.ones((seq, HEAD_DIM), F32)
    z = jnp.zeros((seq, HEAD_DIM), F32)
    return jnp.stack([ident, c]), jnp.stack([z, sa]), jnp.stack([z, sb])


def _qkv_proj(x, g, w, head_gains, seq, *, tm, tn):
    t, d = x.shape
    n = w.shape[1]
    sec_w = n // 6
    c, sa, sb = _rope_tables(seq)
    n_s = seq // tm
    kern = functools.partial(_qkv_kernel, tn=tn, sec_w=sec_w)
    per_sec = sec_w // tn
    return pl.pallas_call(
        kern,
        out_shape=jax.ShapeDtypeStruct((t, n), BF16),
        grid=(t // tm, n // tn),
        in_specs=[
            pl.BlockSpec((tm, d), lambda i, j: (i, 0)),
            pl.BlockSpec((1, d), lambda i, j: (0, 0)),
            pl.BlockSpec((d, tn), lambda i, j: (0, j)),
            pl.BlockSpec((1, 1, HEAD_DIM), lambda i, j: (j // per_sec, 0, 0)),
            pl.BlockSpec((1, tm, HEAD_DIM), lambda i, j: ((j // per_sec) // 3, i % n_s, 0)),
            pl.BlockSpec((1, tm, HEAD_DIM), lambda i, j: ((j // per_sec) // 3, i % n_s, 0)),
            pl.BlockSpec((1, tm, HEAD_DIM), lambda i, j: ((j // per_sec) // 3, i % n_s, 0)),
        ],
        out_specs=pl.BlockSpec((tm, tn), lambda i, j: (i, j)),
        scratch_shapes=[pltpu.VMEM((tm, d), BF16)],
        compiler_params=_params(("parallel", "arbitrary")),
        name="qkv_proj",
    )(x, g.reshape(1, d), w, head_gains, c, sa, sb)


def _matmul_res_kernel(*refs, n_a):
    a_refs = refs[:n_a]
    w_ref, r_ref, o_ref = refs[n_a:]
    acc = r_ref[...]
    k0 = 0
    for a_ref in a_refs:
        k = a_ref.shape[1]
        acc = acc + jnp.dot(a_ref[...], w_ref[k0:k0 + k, :].astype(BF16), preferred_element_type=F32)
        k0 += k
    o_ref[...] = acc


def _matmul_res(a_list, w, res, *, tm, tn):
    t = res.shape[0]
    k, n = w.shape
    kern = functools.partial(_matmul_res_kernel, n_a=len(a_list))
    a_specs = [pl.BlockSpec((tm, a.shape[1]), lambda i, j: (i, 0)) for a in a_list]
    return pl.pallas_call(
        kern,
        out_shape=jax.ShapeDtypeStruct((t, n), F32),
        grid=(t // tm, n // tn),
        in_specs=a_specs + [
            pl.BlockSpec((k, tn), lambda i, j: (0, j)),
            pl.BlockSpec((tm, tn), lambda i, j: (i, j)),
        ],
        out_specs=pl.BlockSpec((tm, tn), lambda i, j: (i, j)),
        compiler_params=_params(("parallel", "arbitrary")),
        name="matmul_res",
    )(*a_list, w, res)


def _silu_mul(gate, up):
    return gate * (1.0 / (1.0 + jnp.exp(-gate))) * up


def _ffn_kernel(x_ref, g_ref, wg_ref, wu_ref, wd_ref, o_ref, hn_ref):
    @pl.when(pl.program_id(1) == 0)
    def _():
        x = x_ref[...]
        hn_ref[...] = _rms(x, g_ref[...]).astype(BF16)
        o_ref[...] = x

    hn = hn_ref[...]
    gate = jnp.dot(hn, wg_ref[...].astype(BF16), preferred_element_type=F32)
    up = jnp.dot(hn, wu_ref[...].astype(BF16), preferred_element_type=F32)
    h = _silu_mul(gate, up).astype(BF16)
    o_ref[...] += jnp.dot(h, wd_ref[...].astype(BF16), preferred_element_type=F32)


def _ffn(x, g, wg, wu, wd, *, tm, tf):
    t, d = x.shape
    dff = wg.shape[1]
    return pl.pallas_call(
        _ffn_kernel,
        out_shape=jax.ShapeDtypeStruct((t, d), F32),
        grid=(t // tm, dff // tf),
        in_specs=[
            pl.BlockSpec((tm, d), lambda i, f: (i, 0), pipeline_mode=pl.Buffered(1)),
            pl.BlockSpec((1, d), lambda i, f: (0, 0)),
            pl.BlockSpec((d, tf), lambda i, f: (0, f)),
            pl.BlockSpec((d, tf), lambda i, f: (0, f)),
            pl.BlockSpec((tf, d), lambda i, f: (f, 0)),
        ],
        out_specs=pl.BlockSpec((tm, d), lambda i, f: (i, 0)),
        scratch_shapes=[pltpu.VMEM((tm, d), BF16)],
        compiler_params=_params(("parallel", "arbitrary")),
        name="ffn_swiglu",
    )(x, g.reshape(1, d), wg, wu, wd)


def _transpose_v(v_ref, vt_ref, blk):
    for c in range(v_ref.shape[0] // blk):
        vt_ref[:, c * blk:(c + 1) * blk] = v_ref[c * blk:(c + 1) * blk, :].astype(F32).T.astype(vt_ref.dtype)


SB_DEAD_LOG_WEIGHT = -110.0


def _sb_kernel(q_ref, k_ref, v_ref, o_ref, vt_ref, acc_ref, *, blk):
    qi = pl.program_id(2)

    @pl.when(qi == 0)
    def _():
        _transpose_v(v_ref, vt_ref, blk)

    q = q_ref[...]
    key = lax.broadcasted_iota(jnp.int32, (blk, blk), 0)
    qry = lax.broadcasted_iota(jnp.int32, (blk, blk), 1)
    later_mat = jnp.where(qry > key, 1.0, 0.0).astype(BF16)

    def log_weights(kj, diagonal):
        start = pl.multiple_of(kj * blk, blk)
        z = lax.dot_general(k_ref[pl.ds(start, blk), :], q, (((1,), (1,)), ((), ())),
                            preferred_element_type=F32)
        sp = jnp.log(1.0 + jnp.exp(-jnp.abs(z)))
        log_beta = jnp.minimum(z, 0.0) - sp
        log_keep = log_beta - z
        if diagonal:
            past = key < qry
            log_keep = jnp.where(past, log_keep, 0.0)
        lk_hi = log_keep.astype(BF16)
        lk_lo = (log_keep - lk_hi.astype(F32)).astype(BF16)
        later = (jnp.dot(later_mat, lk_hi, preferred_element_type=F32)
                 + jnp.dot(later_mat, lk_lo, preferred_element_type=F32))
        lw = log_beta + later
        if diagonal:
            lw = jnp.where(past, lw, NEG_INF)
        return start, lw, jnp.sum(log_keep, axis=0, keepdims=True)

    def weighted_values(start, lw, c):
        w = jnp.exp(lw + c).astype(BF16)
        return jnp.dot(vt_ref[:, pl.ds(start, blk)], w, preferred_element_type=F32)

    has_prev = qi > 0
    st0, lw0, sum0 = log_weights(qi, True)
    st1, lw1, sum1 = log_weights(jnp.maximum(qi - 1, 0), False)
    acc_ref[...] = (weighted_values(st0, lw0, jnp.zeros((1, blk), F32))
                    + weighted_values(st1, lw1, jnp.where(has_prev, sum0, NEG_INF)))
    c = sum0 + jnp.where(has_prev, sum1, 0.0)

    def cond(carry):
        kj, _, alive = carry
        return jnp.logical_and(kj >= 0, alive)

    def body(carry):
        kj, c, _ = carry
        st, lw, sm = log_weights(kj, False)
        acc_ref[...] += weighted_values(st, lw, c)
        c = c + sm
        return kj - 1, c, jnp.max(c) > SB_DEAD_LOG_WEIGHT

    lax.while_loop(cond, body, (qi - 2, c, jnp.max(c) > SB_DEAD_LOG_WEIGHT))
    o_ref[...] = acc_ref[...].T.astype(o_ref.dtype)


def _sb_attention(qkv, batch, seq, *, blk):
    t = qkv.shape[0]
    bq = blk
    nq = seq // bq
    kern = functools.partial(_sb_kernel, blk=blk)
    return pl.pallas_call(
        kern,
        out_shape=jax.ShapeDtypeStruct((t, N_HEADS_SB * HEAD_DIM), BF16),
        grid=(batch, N_HEADS_SB, nq),
        in_specs=[
            pl.BlockSpec((bq, HEAD_DIM), lambda b, h, i: (b * nq + i, h)),
            pl.BlockSpec((seq, HEAD_DIM), lambda b, h, i: (b, N_HEADS_SB + h)),
            pl.BlockSpec((seq, HEAD_DIM), lambda b, h, i: (b, 2 * N_HEADS_SB + h)),
        ],
        out_specs=pl.BlockSpec((bq, HEAD_DIM), lambda b, h, i: (b * nq + i, h)),
        scratch_shapes=[pltpu.VMEM((HEAD_DIM, seq), BF16), pltpu.VMEM((HEAD_DIM, bq), F32)],
        compiler_params=_params(("parallel", "parallel", "arbitrary")),
        name="sb_attention",
    )(qkv, qkv, qkv)


def _moba_kernel(q_ref, k_ref, v_ref, o_ref, vt_ref, kmean_ref, s_ref, *, blk, seq):
    qi = pl.program_id(2)
    n_blocks = seq // blk

    @pl.when(qi == 0)
    def _():
        _transpose_v(v_ref, vt_ref, blk)
        r = lax.broadcasted_iota(jnp.int32, (n_blocks, seq), 0)
        s = lax.broadcasted_iota(jnp.int32, (n_blocks, seq), 1)
        avg = jnp.where(s // blk == r, 1.0 / blk, 0.0).astype(BF16)
        kmean_ref[...] = jnp.dot(avg, k_ref[...], preferred_element_type=F32)

    q = q_ref[...]
    gate = lax.dot_general(kmean_ref[...], q.astype(F32), (((1,), (1,)), ((), ())),
                           precision=lax.Precision.HIGHEST, preferred_element_type=F32)
    blk_id = lax.broadcasted_iota(jnp.int32, (n_blocks, blk), 0)
    lowest = jnp.float32(-3.0e38)
    is_past = blk_id < qi
    g = jnp.where(is_past, gate, lowest)
    bias = jnp.full((n_blocks, blk), NEG_INF, F32)
    for _ in range(MOBA_TOPK):
        mx = jnp.max(g, axis=0, keepdims=True)
        first = jnp.min(jnp.where(g == mx, blk_id, n_blocks), axis=0, keepdims=True)
        pick = blk_id == first
        bias = jnp.where(jnp.logical_and(pick, is_past), 0.0, bias)
        g = jnp.where(pick, lowest, g)
    bias = jnp.where(blk_id == qi, 0.0, bias)

    s = lax.dot_general(k_ref[...], q, (((1,), (1,)), ((), ())), preferred_element_type=F32)
    for n in range(n_blocks):
        s_ref[n * blk:(n + 1) * blk, :] = s[n * blk:(n + 1) * blk, :] + bias[n:n + 1, :]
    own = pl.ds(pl.multiple_of(qi * blk, blk), blk)
    key = lax.broadcasted_iota(jnp.int32, (blk, blk), 0)
    qry = lax.broadcasted_iota(jnp.int32, (blk, blk), 1)
    s_ref[own, :] = jnp.where(key <= qry, s_ref[own, :], NEG_INF)
    s = s_ref[...]
    m = jnp.max(s, axis=0, keepdims=True)
    p = jnp.exp(s - m)
    l = jnp.sum(p, axis=0, keepdims=True)
    acc = jnp.dot(vt_ref[...], p.astype(BF16), preferred_element_type=F32)
    o_ref[...] = (acc / l).T.astype(o_ref.dtype)


def _moba_attention(qkv, batch, seq):
    t = qkv.shape[0]
    blk = MOBA_BLOCK
    nq = seq // blk
    col0 = 3 * N_HEADS_SB
    kern = functools.partial(_moba_kernel, blk=blk, seq=seq)
    return pl.pallas_call(
        kern,
        out_shape=jax.ShapeDtypeStruct((t, N_HEADS_MOBA * HEAD_DIM), BF16),
        grid=(batch, N_HEADS_MOBA, nq),
        in_specs=[
            pl.BlockSpec((blk, HEAD_DIM), lambda b, h, i: (b * nq + i, col0 + h)),
            pl.BlockSpec((seq, HEAD_DIM), lambda b, h, i: (b, col0 + N_HEADS_MOBA + h)),
            pl.BlockSpec((seq, HEAD_DIM), lambda b, h, i: (b, col0 + 2 * N_HEADS_MOBA + h)),
        ],
        out_specs=pl.BlockSpec((blk, HEAD_DIM), lambda b, h, i: (b * nq + i, h)),
        scratch_shapes=[
            pltpu.VMEM((HEAD_DIM, seq), BF16),
            pltpu.VMEM((seq // blk, HEAD_DIM), F32),
            pltpu.VMEM((seq, blk), F32),
        ],
        compiler_params=_params(("parallel", "parallel", "arbitrary")),
        name="moba_attention",
    )(qkv, qkv, qkv)


def _attention_layer(x, seq, batch, attn_norm, w_in, qn_sb, kn_sb, qn_mb, kn_mb, w_o,
                     ffn_norm, w_gate, w_up, w_down, *, tm):
    scale = HEAD_DIM ** -0.5
    ones = jnp.ones((HEAD_DIM,), F32)
    head_gains = jnp.stack([qn_sb * scale, kn_sb, ones, qn_mb * scale, kn_mb, ones]).reshape(6, 1, HEAD_DIM)
    qkv = _qkv_proj(x, attn_norm, w_in, head_gains, seq, tm=tm, tn=512)
    o_sb = _sb_attention(qkv, batch, seq, blk=256)
    o_mb = _moba_attention(qkv, batch, seq)
    x = _matmul_res([o_sb, o_mb], w_o, x, tm=tm, tn=512)
    return _ffn(x, ffn_norm, w_gate, w_up, w_down, tm=tm, tf=256)


def _sigmoid(x):
    return 1.0 / (1.0 + jnp.exp(-x))


def _gelu_tanh(x):
    return 0.5 * x * (1.0 + jnp.tanh(math.sqrt(2.0 / math.pi) * (x + 0.044715 * (x * x * x))))


def _rglru_kernel(xb_ref, gt_ref, cw_ref, cb_ref, wa_ref, ba_ref, wi_ref, bi_ref, lam_ref, o_ref,
                  xpad_ref, h_ref, a_ref, b_ref, *, tt, pad):
    ti = pl.program_id(2)

    @pl.when(ti == 0)
    def _():
        xpad_ref[0:SUBLANES, :] = jnp.zeros((SUBLANES, xpad_ref.shape[1]), F32)
        h_ref[...] = jnp.zeros_like(h_ref)
        a_ref[0:pad, :] = jnp.ones((pad, a_ref.shape[1]), F32)
        b_ref[0:pad, :] = jnp.zeros((pad, b_ref.shape[1]), F32)

    xb = xb_ref[...].astype(F32)
    xpad_ref[SUBLANES:SUBLANES + tt, :] = xb
    cw = cw_ref[...]
    xc = cb_ref[...] + xb * cw[CONV_WIDTH - 1:CONV_WIDTH, :]
    for tap in range(CONV_WIDTH - 1):
        off = SUBLANES - (CONV_WIDTH - 1) + tap
        xc = xc + xpad_ref[off:off + tt, :] * cw[tap:tap + 1, :]
    xpad_ref[0:SUBLANES, :] = xpad_ref[tt:tt + SUBLANES, :]

    xcb = xc.astype(BF16)
    r = _sigmoid(jnp.dot(xcb, wa_ref[0].astype(BF16), preferred_element_type=F32) + ba_ref[...])
    ig = _sigmoid(jnp.dot(xcb, wi_ref[0].astype(BF16), preferred_element_type=F32) + bi_ref[...])
    neg_lam = -lam_ref[...]
    sp = jnp.maximum(neg_lam, 0.0) + jnp.log1p(jnp.exp(-jnp.abs(neg_lam)))
    log_a = (-LRU_C * r) * sp
    a = jnp.exp(log_a)
    u = jnp.sqrt(-jnp.tanh(log_a) * (a * a + 1.0)) * (ig * xc)

    a_ref[pad:pad + tt, :] = a
    b_ref[pad:pad + tt, :] = u
    s = 1
    while s < tt:
        a_cur = a_ref[pad:pad + tt, :]
        b_cur = b_ref[pad:pad + tt, :]
        a_sh = a_ref[pad - s:pad - s + tt, :]
        b_sh = b_ref[pad - s:pad - s + tt, :]
        b_ref[pad:pad + tt, :] = a_cur * b_sh + b_cur
        a_ref[pad:pad + tt, :] = a_cur * a_sh
        s *= 2
    h = a_ref[pad:pad + tt, :] * h_ref[0:1, :] + b_ref[pad:pad + tt, :]
    h_ref[0:1, :] = h[tt - 1:tt, :]
    o_ref[...] = (h * _gelu_tanh(gt_ref[...].astype(F32))).astype(o_ref.dtype)


def _rglru(xbg, batch, seq, conv_w, conv_b, w_a, b_a, w_i, b_i, lam, *, tt):
    t = xbg.shape[0]
    width = xbg.shape[1] // 2
    bw = width // LRU_BLOCKS
    n_t = seq // tt
    pad = tt // 2
    kern = functools.partial(_rglru_kernel, tt=tt, pad=pad)
    vec = lambda a: a.reshape(1, width)
    vspec = pl.BlockSpec((1, bw), lambda b, c, i: (0, c))
    return pl.pallas_call(
        kern,
        out_shape=jax.ShapeDtypeStruct((t, width), BF16),
        grid=(batch, LRU_BLOCKS, n_t),
        in_specs=[
            pl.BlockSpec((tt, bw), lambda b, c, i: (b * n_t + i, c)),
            pl.BlockSpec((tt, bw), lambda b, c, i: (b * n_t + i, LRU_BLOCKS + c)),
            pl.BlockSpec((CONV_WIDTH, bw), lambda b, c, i: (0, c)),
            vspec,
            pl.BlockSpec((1, bw, bw), lambda b, c, i: (c, 0, 0)),
            vspec,
            pl.BlockSpec((1, bw, bw), lambda b, c, i: (c, 0, 0)),
            vspec,
            vspec,
        ],
        out_specs=pl.BlockSpec((tt, bw), lambda b, c, i: (b * n_t + i, c)),
        scratch_shapes=[
            pltpu.VMEM((tt + 2 * SUBLANES, bw), F32),
            pltpu.VMEM((SUBLANES, bw), F32),
            pltpu.VMEM((pad + tt, bw), F32),
            pltpu.VMEM((pad + tt, bw), F32),
        ],
        compiler_params=_params(("parallel", "parallel", "arbitrary")),
        name="rglru",
    )(xbg, xbg, conv_w, vec(conv_b), w_a, vec(b_a), w_i, vec(b_i), vec(lam))


def _recurrent_block(x, seq, batch, rec_norm, w_in, conv_w, conv_b, w_a, b_a, w_i, b_i, lam, w_o, *, tm):
    xbg = _norm_matmul(x, rec_norm, w_in, tm=tm, tn=512)
    y = _rglru(xbg, batch, seq, conv_w, conv_b, w_a, b_a, w_i, b_i, lam, tt=min(512, seq))
    return _matmul_res([y], w_o, x, tm=tm, tn=512)


def _router_kernel(x_ref, g_ref, wr_ref, idx_ref, wt_ref, cnt_ref, run_ref, *, rows):
    @pl.when(pl.program_id(0) == 0)
    def _():
        run_ref[...] = jnp.zeros_like(run_ref)

    hn = _rms(x_ref[...], g_ref[...])
    logits = jnp.dot(hn, wr_ref[...], precision=lax.Precision.HIGHEST, preferred_element_type=F32)
    lane = lax.broadcasted_iota(jnp.int32, (rows, LANES), 1)
    lowest = jnp.float32(-3.0e38)
    logits = jnp.where(lane < N_EXPERTS, logits, lowest)
    m1 = jnp.max(logits, axis=1, keepdims=True)
    e1 = jnp.min(jnp.where(logits == m1, lane, LANES), axis=1, keepdims=True)
    rest = jnp.where(lane == e1, lowest, logits)
    m2 = jnp.max(rest, axis=1, keepdims=True)
    e2 = jnp.min(jnp.where(rest == m2, lane, LANES), axis=1, keepdims=True)
    ex = jnp.exp(m2 - m1)
    w1 = 1.0 / (1.0 + ex)
    w2 = ex / (1.0 + ex)

    onehot = jnp.where(jnp.logical_or(lane == e1, lane == e2), 1.0, 0.0)
    rr = lax.broadcasted_iota(jnp.int32, (rows, rows), 0)
    cc = lax.broadcasted_iota(jnp.int32, (rows, rows), 1)
    before = jnp.where(cc < rr, 1.0, 0.0).astype(BF16)
    rank = jnp.dot(before, onehot.astype(BF16), preferred_element_type=F32) + run_ref[0:1, :]
    r1 = jnp.sum(jnp.where(lane == e1, rank, 0.0), axis=1, keepdims=True).astype(jnp.int32)
    r2 = jnp.sum(jnp.where(lane == e2, rank, 0.0), axis=1, keepdims=True).astype(jnp.int32)
    run_ref[0:1, :] = run_ref[0:1, :] + jnp.sum(onehot, axis=0, keepdims=True)

    packed = jnp.where(lane == 0, e1, jnp.where(lane == 1, e2, jnp.where(lane == 2, r1, r2)))
    idx_ref[...] = packed[:, 0:idx_ref.shape[1]]
    wts = jnp.where(lane == 0, w1, w2)
    wt_ref[...] = wts[:, 0:wt_ref.shape[1]]
    cnt_ref[...] = run_ref[...].astype(jnp.int32)


def _router(x, g, router, *, rows):
    t, d = x.shape
    wr = jnp.pad(router, ((0, 0), (0, LANES - router.shape[1])))
    kern = functools.partial(_router_kernel, rows=rows)
    return pl.pallas_call(
        kern,
        out_shape=(
            jax.ShapeDtypeStruct((t, SUBLANES), jnp.int32),
            jax.ShapeDtypeStruct((t, SUBLANES), F32),
            jax.ShapeDtypeStruct((SUBLANES, LANES), jnp.int32),
        ),
        grid=(t // rows,),
        in_specs=[
            pl.BlockSpec((rows, d), lambda i: (i, 0)),
            pl.BlockSpec((1, d), lambda i: (0, 0)),
            pl.BlockSpec((d, LANES), lambda i: (0, 0)),
        ],
        out_specs=(
            pl.BlockSpec((rows, SUBLANES), lambda i: (i, 0)),
            pl.BlockSpec((rows, SUBLANES), lambda i: (i, 0)),
            pl.BlockSpec((SUBLANES, LANES), lambda i: (0, 0)),
        ),
        scratch_shapes=[pltpu.VMEM((SUBLANES, LANES), F32)],
        compiler_params=_params(("arbitrary",)),
        name="moe_router",
    )(x, g.reshape(1, d), wr)


def _moe_ffn_kernel(te_ref, nu_ref, tok_ref, x_ref, g_ref, wg_ref, wu_ref, wd_ref, o_ref, xg_ref, hn_ref, sem, *, tm):
    del te_ref
    i = pl.program_id(0)
    first = pl.program_id(1) == 0

    @pl.when(jnp.logical_and(i >= nu_ref[0], first))
    def _():
        o_ref[...] = jnp.zeros_like(o_ref)

    @pl.when(i < nu_ref[0])
    def _():
        @pl.when(first)
        def _():
            def row_copy(r):
                return pltpu.make_async_copy(x_ref.at[pl.ds(tok_ref[r], 1)], xg_ref.at[pl.ds(r, 1)], sem)

            def start(r, carry):
                row_copy(r).start()
                return carry

            def wait(r, carry):
                row_copy(r).wait()
                return carry

            lax.fori_loop(0, tm, start, 0)
            lax.fori_loop(0, tm, wait, 0)
            hn_ref[...] = _rms(xg_ref[...], g_ref[...]).astype(BF16)
            o_ref[...] = jnp.zeros_like(o_ref)

        hn = hn_ref[...]
        gate = jnp.dot(hn, wg_ref[0].astype(BF16), preferred_element_type=F32)
        up = jnp.dot(hn, wu_ref[0].astype(BF16), preferred_element_type=F32)
        h = _silu_mul(gate, up).astype(BF16)
        o_ref[...] += jnp.dot(h, wd_ref[0].astype(BF16), preferred_element_type=F32)


def _moe_ffn(x, tok, g, we_gate, we_up, we_down, tile_expert, n_used, *, tm, tf):
    d = x.shape[1]
    n_rows = tok.shape[0]
    dff = we_gate.shape[2]
    n_f = dff // tf
    n_tiles = n_rows // tm

    def tile(i, nu):
        return jnp.minimum(i, nu[0] - 1)

    def fidx(i, f, nu):
        return jnp.where(i < nu[0], f, n_f - 1)

    grid_spec = pltpu.PrefetchScalarGridSpec(
        num_scalar_prefetch=2,
        grid=(n_tiles, n_f),
        in_specs=[
            pl.BlockSpec((tm,), lambda i, f, te, nu: (tile(i, nu),), memory_space=pltpu.SMEM),
            pl.BlockSpec(memory_space=pl.ANY),
            pl.BlockSpec((1, d), lambda i, f, te, nu: (0, 0)),
            pl.BlockSpec((1, d, tf), lambda i, f, te, nu: (te[tile(i, nu)], 0, fidx(i, f, nu))),
            pl.BlockSpec((1, d, tf), lambda i, f, te, nu: (te[tile(i, nu)], 0, fidx(i, f, nu))),
            pl.BlockSpec((1, tf, d), lambda i, f, te, nu: (te[tile(i, nu)], fidx(i, f, nu), 0)),
        ],
        out_specs=pl.BlockSpec((tm, d), lambda i, f, te, nu: (i, 0)),
        scratch_shapes=[
            pltpu.VMEM((tm, d), F32),
            pltpu.VMEM((tm, d), BF16),
            pltpu.SemaphoreType.DMA(()),
        ],
    )
    return pl.pallas_call(
        functools.partial(_moe_ffn_kernel, tm=tm),
        out_shape=jax.ShapeDtypeStruct((n_rows, d), F32),
        grid_spec=grid_spec,
        compiler_params=_params(("arbitrary", "arbitrary")),
        name="moe_ffn",
    )(tile_expert, n_used, tok, x, g.reshape(1, d), we_gate, we_up, we_down)


COMBINE_ROWS = 512


def _combine_kernel(pos_ref, x_ref, wt_ref, ys_ref, o_ref, y1_ref, y2_ref, sem):
    def row_copy(r, k, dst):
        return pltpu.make_async_copy(ys_ref.at[pl.ds(pos_ref[2 * r + k], 1)], dst.at[pl.ds(r, 1)], sem)

    def start(r, carry):
        row_copy(r, 0, y1_ref).start()
        row_copy(r, 1, y2_ref).start()
        return carry

    def wait(r, carry):
        row_copy(r, 0, y1_ref).wait()
        row_copy(r, 1, y2_ref).wait()
        return carry

    lax.fori_loop(0, COMBINE_ROWS, start, 0)
    lax.fori_loop(0, COMBINE_ROWS, wait, 0)
    wt = wt_ref[...]
    o_ref[...] = x_ref[...] + wt[:, 0:1] * y1_ref[...] + wt[:, 1:2] * y2_ref[...]


def _combine(x, wts, ys, pos_flat):
    t, d = x.shape
    return pl.pallas_call(
        _combine_kernel,
        out_shape=jax.ShapeDtypeStruct((t, d), F32),
        grid=(t // COMBINE_ROWS,),
        in_specs=[
            pl.BlockSpec((2 * COMBINE_ROWS,), lambda i: (i,), memory_space=pltpu.SMEM),
            pl.BlockSpec((COMBINE_ROWS, d), lambda i: (i, 0)),
            pl.BlockSpec((COMBINE_ROWS, SUBLANES), lambda i: (i, 0)),
            pl.BlockSpec(memory_space=pl.ANY),
        ],
        out_specs=pl.BlockSpec((COMBINE_ROWS, d), lambda i: (i, 0)),
        scratch_shapes=[
            pltpu.VMEM((COMBINE_ROWS, d), F32),
            pltpu.VMEM((COMBINE_ROWS, d), F32),
            pltpu.SemaphoreType.DMA(()),
        ],
        compiler_params=_params(("arbitrary",)),
        name="moe_combine",
    )(pos_flat, x, wts, ys)


def _moe_block(x, ffn_norm, router, we_gate, we_up, we_down, *, tm=1024, tf=256):
    t, d = x.shape
    idx, wts, cnt = _router(x, ffn_norm, router, rows=256)
    n_tiles = (TOP_K * t) // tm + N_EXPERTS
    counts = cnt[0, :N_EXPERTS]
    padded = ((counts + tm - 1) // tm) * tm
    ends = jnp.cumsum(padded)
    offsets = ends - padded
    pos = jnp.take(offsets, idx[:, 0:2]) + idx[:, 2:4]
    pos_flat = pos.reshape(-1).astype(jnp.int32)
    tile_start = jnp.arange(n_tiles, dtype=jnp.int32) * tm
    tile_expert = jnp.minimum(jnp.sum(tile_start[:, None] >= ends[None, :], axis=1), N_EXPERTS - 1).astype(jnp.int32)
    n_used = (ends[-1:] // tm).astype(jnp.int32)
    tok = jnp.zeros((n_tiles * tm,), jnp.int32).at[pos_flat].set(jnp.repeat(jnp.arange(t, dtype=jnp.int32), TOP_K))
    ys = _moe_ffn(x, tok, ffn_norm, we_gate, we_up, we_down, tile_expert, n_used, tm=tm, tf=tf)
    return _combine(x, wts, ys, pos_flat)


def kernel(x, ev_attn_norm, ev_w_in, ev_q_norm_sb, ev_k_norm_sb, ev_q_norm_moba, ev_k_norm_moba, ev_w_o, ev_ffn_norm, ev_w_gate, ev_w_up, ev_w_down, od_rec_norm, od_w_in, od_conv_w, od_conv_b, od_w_a, od_b_a, od_w_i, od_b_i, od_lambda, od_w_o, od_ffn_norm, od_router, od_we_gate, od_we_up, od_we_down):
    batch, seq, d = x.shape
    t = batch * seq
    tm = min(1024, seq)
    h = x.reshape(t, d)
    h = _attention_layer(h, seq, batch, ev_attn_norm[0], ev_w_in[0], ev_q_norm_sb[0], ev_k_norm_sb[0],
                         ev_q_norm_moba[0], ev_k_norm_moba[0], ev_w_o[0], ev_ffn_norm[0],
                         ev_w_gate[0], ev_w_up[0], ev_w_down[0], tm=tm)
    h = _recurrent_block(h, seq, batch, od_rec_norm[0], od_w_in[0], od_conv_w[0], od_conv_b[0], od_w_a[0],
                         od_b_a[0], od_w_i[0], od_b_i[0], od_lambda[0], od_w_o[0], tm=tm)
    h = _moe_block(h, od_ffn_norm[0], od_router[0], od_we_gate[0], od_we_up[0], od_we_down[0])
    return h.reshape(batch, seq, d)
```

```python
import functools
import math

import jax
import jax.numpy as jnp
from jax import lax
from jax.experimental import pallas as pl
from jax.experimental.pallas import tpu as pltpu

F32 = jnp.float32
BF16 = jnp.bfloat16

HEAD_DIM = 128
N_HEADS_SB = 8
N_HEADS_MOBA = 8
ROPE_DIM = HEAD_DIM // 4
ROPE_THETA = 500000.0
MOBA_BLOCK = 256
MOBA_TOPK = 3
LRU_BLOCKS = 8
LRU_C = 8.0
CONV_WIDTH = 4
N_EXPERTS = 8
TOP_K = 2
NORM_EPS = 1e-6
NEG_INF = -1e30

LANES = 128
SUBLANES = 8
VMEM_LIMIT_BYTES = 56 * 1024 * 1024


def _params(semantics, vmem=VMEM_LIMIT_BYTES):
    return pltpu.CompilerParams(dimension_semantics=semantics, vmem_limit_bytes=vmem)


def _rms(x, g):
    ms = jnp.mean(x * x, axis=-1, keepdims=True)
    return x * lax.rsqrt(ms + NORM_EPS) * g


def _norm_matmul_kernel(x_ref, g_ref, w_ref, o_ref, hn_ref):
    @pl.when(pl.program_id(1) == 0)
    def _():
        hn_ref[...] = _rms(x_ref[...], g_ref[...]).astype(BF16)

    y = jnp.dot(hn_ref[...], w_ref[...].astype(BF16), preferred_element_type=F32)
    o_ref[...] = y.astype(o_ref.dtype)


def _qkv_kernel(x_ref, g_ref, w_ref, hg_ref, c_ref, sa_ref, sb_ref, o_ref, hn_ref, *, tn, sec_w):
    j = pl.program_id(1)

    @pl.when(j == 0)
    def _():
        hn_ref[...] = _rms(x_ref[...], g_ref[...]).astype(BF16)

    y = jnp.dot(hn_ref[...], w_ref[...].astype(BF16), preferred_element_type=F32)
    sec = j // (sec_w // tn)
    is_v = jnp.logical_or(sec == 2, sec == 5)

    @pl.when(is_v)
    def _():
        o_ref[...] = y.astype(o_ref.dtype)

    @pl.when(jnp.logical_not(is_v))
    def _():
        gain = hg_ref[0]
        cos = c_ref[0]
        sin_a = sa_ref[0]
        sin_b = sb_ref[0]
        for hh in range(tn // HEAD_DIM):
            t = _rms(y[:, hh * HEAD_DIM:(hh + 1) * HEAD_DIM], gain)
            t = (t * cos + pltpu.roll(t, ROPE_DIM // 2, 1) * sin_a
                 + pltpu.roll(t, HEAD_DIM - ROPE_DIM // 2, 1) * sin_b)
            o_ref[:, hh * HEAD_DIM:(hh + 1) * HEAD_DIM] = t.astype(o_ref.dtype)


def _norm_matmul(x, g, w, *, tm, tn, out_dtype=BF16):
    t, d = x.shape
    n = w.shape[1]
    return pl.pallas_call(
        _norm_matmul_kernel,
        out_shape=jax.ShapeDtypeStruct((t, n), out_dtype),
        grid=(t // tm, n // tn),
        in_specs=[
            pl.BlockSpec((tm, d), lambda i, j: (i, 0)),
            pl.BlockSpec((1, d), lambda i, j: (0, 0)),
            pl.BlockSpec((d, tn), lambda i, j: (0, j)),
        ],
        out_specs=pl.BlockSpec((tm, tn), lambda i, j: (i, j)),
        scratch_shapes=[pltpu.VMEM((tm, d), BF16)],
        compiler_params=_params(("parallel", "arbitrary")),
        name="norm_matmul",
    )(x, g.reshape(1, d), w)


def _rope_tables(seq):
    half = ROPE_DIM // 2
    inv_freq = ROPE_THETA ** (-jnp.arange(0, ROPE_DIM, 2, dtype=F32) / ROPE_DIM)
    ang = jnp.arange(seq, dtype=F32)[:, None] * inv_freq[None, :]
    cos, sin = jnp.cos(ang), jnp.sin(ang)
    ones = jnp.ones((seq, HEAD_DIM - ROPE_DIM), F32)
    zeros = jnp.zeros((seq, HEAD_DIM - ROPE_DIM), F32)
    zh = jnp.zeros((seq, half), F32)
    c = jnp.concatenate([cos, cos, ones], axis=1)
    sa = jnp.concatenate([zh, sin, zeros], axis=1)
    sb = jnp.concatenate([-sin, zh, zeros], axis=1)
    ident = jnp.ones((seq, HEAD_DIM), F32)
    z = jnp.zeros((seq, HEAD_DIM), F32)
    return jnp.stack([ident, c]), jnp.stack([z, sa]), jnp.stack([z, sb])


def _qkv_proj(x, g, w, head_gains, seq, *, tm, tn):
    t, d = x.shape
    n = w.shape[1]
    sec_w = n // 6
    c, sa, sb = _rope_tables(seq)
    n_s = seq // tm
    kern = functools.partial(_qkv_kernel, tn=tn, sec_w=sec_w)
    per_sec = sec_w // tn
    return pl.pallas_call(
        kern,
        out_shape=jax.ShapeDtypeStruct((t, n), BF16),
        grid=(t // tm, n // tn),
        in_specs=[
            pl.BlockSpec((tm, d), lambda i, j: (i, 0)),
            pl.BlockSpec((1, d), lambda i, j: (0, 0)),
            pl.BlockSpec((d, tn), lambda i, j: (0, j)),
            pl.BlockSpec((1, 1, HEAD_DIM), lambda i, j: (j // per_sec, 0, 0)),
            pl.BlockSpec((1, tm, HEAD_DIM), lambda i, j: ((j // per_sec) // 3, i % n_s, 0)),
            pl.BlockSpec((1, tm, HEAD_DIM), lambda i, j: ((j // per_sec) // 3, i % n_s, 0)),
            pl.BlockSpec((1, tm, HEAD_DIM), lambda i, j: ((j // per_sec) // 3, i % n_s, 0)),
        ],
        out_specs=pl.BlockSpec((tm, tn), lambda i, j: (i, j)),
        scratch_shapes=[pltpu.VMEM((tm, d), BF16)],
        compiler_params=_params(("parallel", "arbitrary")),
        name="qkv_proj",
    )(x, g.reshape(1, d), w, head_gains, c, sa, sb)


def _matmul_res_kernel(*refs, n_a):
    a_refs = refs[:n_a]
    w_ref, r_ref, o_ref = refs[n_a:]
    acc = r_ref[...]
    k0 = 0
    for a_ref in a_refs:
        k = a_ref.shape[1]
        acc = acc + jnp.dot(a_ref[...], w_ref[k0:k0 + k, :].astype(BF16), preferred_element_type=F32)
        k0 += k
    o_ref[...] = acc


def _matmul_res(a_list, w, res, *, tm, tn):
    t = res.shape[0]
    k, n = w.shape
    kern = functools.partial(_matmul_res_kernel, n_a=len(a_list))
    a_specs = [pl.BlockSpec((tm, a.shape[1]), lambda i, j: (i, 0)) for a in a_list]
    return pl.pallas_call(
        kern,
        out_shape=jax.ShapeDtypeStruct((t, n), F32),
        grid=(t // tm, n // tn),
        in_specs=a_specs + [
            pl.BlockSpec((k, tn), lambda i, j: (0, j)),
            pl.BlockSpec((tm, tn), lambda i, j: (i, j)),
        ],
        out_specs=pl.BlockSpec((tm, tn), lambda i, j: (i, j)),
        compiler_params=_params(("parallel", "arbitrary")),
        name="matmul_res",
    )(*a_list, w, res)


def _silu_mul(gate, up):
    return gate * (1.0 / (1.0 + jnp.exp(-gate))) * up


def _ffn_kernel(x_ref, g_ref, wg_ref, wu_ref, wd_ref, o_ref, hn_ref):
    @pl.when(pl.program_id(1) == 0)
    def _():
        x = x_ref[...]
        hn_ref[...] = _rms(x, g_ref[...]).astype(BF16)
        o_ref[...] = x

    hn = hn_ref[...]
    gate = jnp.dot(hn, wg_ref[...].astype(BF16), preferred_element_type=F32)
    up = jnp.dot(hn, wu_ref[...].astype(BF16), preferred_element_type=F32)
    h = _silu_mul(gate, up).astype(BF16)
    o_ref[...] += jnp.dot(h, wd_ref[...].astype(BF16), preferred_element_type=F32)


def _ffn(x, g, wg, wu, wd, *, tm, tf):
    t, d = x.shape
    dff = wg.shape[1]
    return pl.pallas_call(
        _ffn_kernel,
        out_shape=jax.ShapeDtypeStruct((t, d), F32),
        grid=(t // tm, dff // tf),
        in_specs=[
            pl.BlockSpec((tm, d), lambda i, f: (i, 0), pipeline_mode=pl.Buffered(1)),
            pl.BlockSpec((1, d), lambda i, f: (0, 0)),
            pl.BlockSpec((d, tf), lambda i, f: (0, f)),
            pl.BlockSpec((d, tf), lambda i, f: (0, f)),
            pl.BlockSpec((tf, d), lambda i, f: (f, 0)),
        ],
        out_specs=pl.BlockSpec((tm, d), lambda i, f: (i, 0)),
        scratch_shapes=[pltpu.VMEM((tm, d), BF16)],
        compiler_params=_params(("parallel", "arbitrary")),
        name="ffn_swiglu",
    )(x, g.reshape(1, d), wg, wu, wd)


def _transpose_v(v_ref, vt_ref, blk):
    for c in range(v_ref.shape[0] // blk):
        vt_ref[:, c * blk:(c + 1) * blk] = v_ref[c * blk:(c + 1) * blk, :].astype(F32).T.astype(vt_ref.dtype)


SB_DEAD_LOG_WEIGHT = -110.0


def _sb_kernel(q_ref, k_ref, v_ref, o_ref, vt_ref, acc_ref, *, blk):
    qi = pl.program_id(2)

    @pl.when(qi == 0)
    def _():
        _transpose_v(v_ref, vt_ref, blk)

    q = q_ref[...]
    key = lax.broadcasted_iota(jnp.int32, (blk, blk), 0)
    qry = lax.broadcasted_iota(jnp.int32, (blk, blk), 1)
    later_mat = jnp.where(qry > key, 1.0, 0.0).astype(BF16)

    def log_weights(kj, diagonal):
        start = pl.multiple_of(kj * blk, blk)
        z = lax.dot_general(k_ref[pl.ds(start, blk), :], q, (((1,), (1,)), ((), ())),
                            preferred_element_type=F32)
        sp = jnp.log(1.0 + jnp.exp(-jnp.abs(z)))
        log_beta = jnp.minimum(z, 0.0) - sp
        log_keep = log_beta - z
        if diagonal:
            past = key < qry
            log_keep = jnp.where(past, log_keep, 0.0)
        lk_hi = log_keep.astype(BF16)
        lk_lo = (log_keep - lk_hi.astype(F32)).astype(BF16)
        later = (jnp.dot(later_mat, lk_hi, preferred_element_type=F32)
                 + jnp.dot(later_mat, lk_lo, preferred_element_type=F32))
        lw = log_beta + later
        if diagonal:
            lw = jnp.where(past, lw, NEG_INF)
        return start, lw, jnp.sum(log_keep, axis=0, keepdims=True)

    def weighted_values(start, lw, c):
        w = jnp.exp(lw + c).astype(BF16)
        return jnp.dot(vt_ref[:, pl.ds(start, blk)], w, preferred_element_type=F32)

    has_prev = qi > 0
    st0, lw0, sum0 = log_weights(qi, True)
    st1, lw1, sum1 = log_weights(jnp.maximum(qi - 1, 0), False)
    acc_ref[...] = (weighted_values(st0, lw0, jnp.zeros((1, blk), F32))
                    + weighted_values(st1, lw1, jnp.where(has_prev, sum0, NEG_INF)))
    c = sum0 + jnp.where(has_prev, sum1, 0.0)

    def cond(carry):
        kj, _, alive = carry
        return jnp.logical_and(kj >= 0, alive)

    def body(carry):
        kj, c, _ = carry
        st, lw, sm = log_weights(kj, False)
        acc_ref[...] += weighted_values(st, lw, c)
        c = c + sm
        return kj - 1, c, jnp.max(c) > SB_DEAD_LOG_WEIGHT

    lax.while_loop(cond, body, (qi - 2, c, jnp.max(c) > SB_DEAD_LOG_WEIGHT))
    o_ref[...] = acc_ref[...].T.astype(o_ref.dtype)


def _sb_attention(qkv, batch, seq, *, blk):
    t = qkv.shape[0]
    bq = blk
    nq = seq // bq
    kern = functools.partial(_sb_kernel, blk=blk)
    return pl.pallas_call(
        kern,
        out_shape=jax.ShapeDtypeStruct((t, N_HEADS_SB * HEAD_DIM), BF16),
        grid=(batch, N_HEADS_SB, nq),
        in_specs=[
            pl.BlockSpec((bq, HEAD_DIM), lambda b, h, i: (b * nq + i, h)),
            pl.BlockSpec((seq, HEAD_DIM), lambda b, h, i: (b, N_HEADS_SB + h)),
            pl.BlockSpec((seq, HEAD_DIM), lambda b, h, i: (b, 2 * N_HEADS_SB + h)),
        ],
        out_specs=pl.BlockSpec((bq, HEAD_DIM), lambda b, h, i: (b * nq + i, h)),
        scratch_shapes=[pltpu.VMEM((HEAD_DIM, seq), BF16), pltpu.VMEM((HEAD_DIM, bq), F32)],
        compiler_params=_params(("parallel", "parallel", "arbitrary")),
        name="sb_attention",
    )(qkv, qkv, qkv)


MOBA_GROUP = 4
MOBA_HEADS_PER_STEP = 1


def _moba_kernel(q_ref, k_ref, v_ref, o_ref, vt_ref, kmean_ref, *, blk, seq, heads):
    qi = pl.program_id(2)
    n_blocks = seq // blk
    hd = HEAD_DIM

    @pl.when(qi == 0)
    def _():
        r = lax.broadcasted_iota(jnp.int32, (n_blocks, seq), 0)
        s = lax.broadcasted_iota(jnp.int32, (n_blocks, seq), 1)
        avg = jnp.where(s // blk == r, 1.0 / blk, 0.0).astype(BF16)
        for hh in range(heads):
            for c in range(n_blocks):
                vt_ref[hh, :, c * blk:(c + 1) * blk] = (
                    v_ref[c * blk:(c + 1) * blk, hh * hd:(hh + 1) * hd].astype(F32).T.astype(vt_ref.dtype))
            kmean_ref[hh] = jnp.dot(avg, k_ref[:, hh * hd:(hh + 1) * hd], preferred_element_type=F32)

    blk_id = lax.broadcasted_iota(jnp.int32, (n_blocks, blk), 0)
    key = lax.broadcasted_iota(jnp.int32, (blk, blk), 0)
    qry = lax.broadcasted_iota(jnp.int32, (blk, blk), 1)
    lowest = jnp.float32(-3.0e38)
    is_past = blk_id < qi
    own = pl.multiple_of(qi * blk, blk)

    def scores(hh, q, rows):
        return lax.dot_general(k_ref[rows, hh * hd:(hh + 1) * hd], q, (((1,), (1,)), ((), ())),
                               preferred_element_type=F32)

    state = []
    for hh in range(heads):
        q = q_ref[:, hh * hd:(hh + 1) * hd]
        gate = lax.dot_general(kmean_ref[hh], q.astype(F32), (((1,), (1,)), ((), ())),
                               precision=lax.Precision.HIGHEST, preferred_element_type=F32)
        g = jnp.where(is_past, gate, lowest)
        bias = jnp.full((n_blocks, blk), NEG_INF, F32)
        for _ in range(MOBA_TOPK):
            mx = jnp.max(g, axis=0, keepdims=True)
            first = jnp.min(jnp.where(g == mx, blk_id, n_blocks), axis=0, keepdims=True)
            pick = blk_id == first
            bias = jnp.where(jnp.logical_and(pick, is_past), 0.0, bias)
            g = jnp.where(pick, lowest, g)

        s_own = jnp.where(key <= qry, scores(hh, q, pl.ds(own, blk)), NEG_INF)
        m0 = jnp.max(s_own, axis=0, keepdims=True)
        p0 = jnp.exp(s_own - m0)
        l0 = jnp.sum(p0, axis=0, keepdims=True)
        acc0 = jnp.dot(vt_ref[hh, :, pl.ds(own, blk)], p0.astype(BF16), preferred_element_type=F32)
        state.append((q, bias, m0, l0, acc0))

    def past_blocks(extent):
        for hh in range(heads):
            q, bias, m0, l0, acc0 = state[hh]
            s = scores(hh, q, slice(0, extent * blk))
            s = jnp.concatenate([s[i * blk:(i + 1) * blk, :] + bias[i:i + 1, :] for i in range(extent)], axis=0)
            m = jnp.maximum(m0, jnp.max(s, axis=0, keepdims=True))
            alpha = jnp.exp(m0 - m)
            p = jnp.exp(s - m)
            l = alpha * l0 + jnp.sum(p, axis=0, keepdims=True)
            acc = alpha * acc0 + jnp.dot(vt_ref[hh, :, 0:extent * blk], p.astype(BF16),
                                         preferred_element_type=F32)
            o_ref[:, hh * hd:(hh + 1) * hd] = (acc / l).T.astype(o_ref.dtype)

    n_groups = -(-n_blocks // MOBA_GROUP)
    for grp in range(n_groups):
        @pl.when(qi // MOBA_GROUP == grp)
        def _(grp=grp):
            past_blocks(min((grp + 1) * MOBA_GROUP, n_blocks))


def _moba_attention(qkv, batch, seq, *, heads=MOBA_HEADS_PER_STEP):
    t = qkv.shape[0]
    blk = MOBA_BLOCK
    nq = seq // blk
    width = heads * HEAD_DIM
    col0 = 3 * N_HEADS_SB // heads
    nh = N_HEADS_MOBA // heads
    kern = functools.partial(_moba_kernel, blk=blk, seq=seq, heads=heads)
    return pl.pallas_call(
        kern,
        out_shape=jax.ShapeDtypeStruct((t, N_HEADS_MOBA * HEAD_DIM), BF16),
        grid=(batch, nh, nq),
        in_specs=[
            pl.BlockSpec((blk, width), lambda b, h, i: (b * nq + i, col0 + h)),
            pl.BlockSpec((seq, width), lambda b, h, i: (b, col0 + nh + h)),
            pl.BlockSpec((seq, width), lambda b, h, i: (b, col0 + 2 * nh + h)),
        ],
        out_specs=pl.BlockSpec((blk, width), lambda b, h, i: (b * nq + i, h)),
        scratch_shapes=[
            pltpu.VMEM((heads, HEAD_DIM, seq), BF16),
            pltpu.VMEM((heads, seq // blk, HEAD_DIM), F32),
        ],
        compiler_params=_params(("parallel", "parallel", "arbitrary")),
        name="moba_attention",
    )(qkv, qkv, qkv)


def _attention_layer(x, seq, batch, attn_norm, w_in, qn_sb, kn_sb, qn_mb, kn_mb, w_o,
                     ffn_norm, w_gate, w_up, w_down, *, tm):
    scale = HEAD_DIM ** -0.5
    ones = jnp.ones((HEAD_DIM,), F32)
    head_gains = jnp.stack([qn_sb * scale, kn_sb, ones, qn_mb * scale, kn_mb, ones]).reshape(6, 1, HEAD_DIM)
    qkv = _qkv_proj(x, attn_norm, w_in, head_gains, seq, tm=tm, tn=512)
    o_sb = _sb_attention(qkv, batch, seq, blk=256)
    o_mb = _moba_attention(qkv, batch, seq)
    x = _matmul_res([o_sb, o_mb], w_o, x, tm=tm, tn=512)
    return _ffn(x, ffn_norm, w_gate, w_up, w_down, tm=tm, tf=256)


def _sigmoid(x):
    return 1.0 / (1.0 + jnp.exp(-x))


def _gelu_tanh(x):
    return 0.5 * x * (1.0 + jnp.tanh(math.sqrt(2.0 / math.pi) * (x + 0.044715 * (x * x * x))))


def _rglru_kernel(xb_ref, gt_ref, cw_ref, cb_ref, wa_ref, ba_ref, wi_ref, bi_ref, lam_ref, o_ref,
                  xpad_ref, h_ref, a_ref, b_ref, *, tt, pad):
    ti = pl.program_id(2)

    @pl.when(ti == 0)
    def _():
        xpad_ref[0:SUBLANES, :] = jnp.zeros((SUBLANES, xpad_ref.shape[1]), F32)
        h_ref[...] = jnp.zeros_like(h_ref)
        a_ref[0:pad, :] = jnp.ones((pad, a_ref.shape[1]), F32)
        b_ref[0:pad, :] = jnp.zeros((pad, b_ref.shape[1]), F32)

    xb = xb_ref[...].astype(F32)
    xpad_ref[SUBLANES:SUBLANES + tt, :] = xb
    cw = cw_ref[...]
    xc = cb_ref[...] + xb * cw[CONV_WIDTH - 1:CONV_WIDTH, :]
    for tap in range(CONV_WIDTH - 1):
        off = SUBLANES - (CONV_WIDTH - 1) + tap
        xc = xc + xpad_ref[off:off + tt, :] * cw[tap:tap + 1, :]
    xpad_ref[0:SUBLANES, :] = xpad_ref[tt:tt + SUBLANES, :]

    xcb = xc.astype(BF16)
    r = _sigmoid(jnp.dot(xcb, wa_ref[0].astype(BF16), preferred_element_type=F32) + ba_ref[...])
    ig = _sigmoid(jnp.dot(xcb, wi_ref[0].astype(BF16), preferred_element_type=F32) + bi_ref[...])
    neg_lam = -lam_ref[...]
    sp = jnp.maximum(neg_lam, 0.0) + jnp.log1p(jnp.exp(-jnp.abs(neg_lam)))
    log_a = (-LRU_C * r) * sp
    a = jnp.exp(log_a)
    u = jnp.sqrt(-jnp.tanh(log_a) * (a * a + 1.0)) * (ig * xc)

    a_ref[pad:pad + tt, :] = a
    b_ref[pad:pad + tt, :] = u
    s = 1
    while s < tt:
        a_cur = a_ref[pad:pad + tt, :]
        b_cur = b_ref[pad:pad + tt, :]
        a_sh = a_ref[pad - s:pad - s + tt, :]
        b_sh = b_ref[pad - s:pad - s + tt, :]
        b_ref[pad:pad + tt, :] = a_cur * b_sh + b_cur
        a_ref[pad:pad + tt, :] = a_cur * a_sh
        s *= 2
    h = a_ref[pad:pad + tt, :] * h_ref[0:1, :] + b_ref[pad:pad + tt, :]
    h_ref[0:1, :] = h[tt - 1:tt, :]
    o_ref[...] = (h * _gelu_tanh(gt_ref[...].astype(F32))).astype(o_ref.dtype)


def _rglru(xbg, batch, seq, conv_w, conv_b, w_a, b_a, w_i, b_i, lam, *, tt):
    t = xbg.shape[0]
    width = xbg.shape[1] // 2
    bw = width // LRU_BLOCKS
    n_t = seq // tt
    pad = tt // 2
    kern = functools.partial(_rglru_kernel, tt=tt, pad=pad)
    vec = lambda a: a.reshape(1, width)
    vspec = pl.BlockSpec((1, bw), lambda b, c, i: (0, c))
    return pl.pallas_call(
        kern,
        out_shape=jax.ShapeDtypeStruct((t, width), BF16),
        grid=(batch, LRU_BLOCKS, n_t),
        in_specs=[
            pl.BlockSpec((tt, bw), lambda b, c, i: (b * n_t + i, c)),
            pl.BlockSpec((tt, bw), lambda b, c, i: (b * n_t + i, LRU_BLOCKS + c)),
            pl.BlockSpec((CONV_WIDTH, bw), lambda b, c, i: (0, c)),
            vspec,
            pl.BlockSpec((1, bw, bw), lambda b, c, i: (c, 0, 0)),
            vspec,
            pl.BlockSpec((1, bw, bw), lambda b, c, i: (c, 0, 0)),
            vspec,
            vspec,
        ],
        out_specs=pl.BlockSpec((tt, bw), lambda b, c, i: (b * n_t + i, c)),
        scratch_shapes=[
            pltpu.VMEM((tt + 2 * SUBLANES, bw), F32),
            pltpu.VMEM((SUBLANES, bw), F32),
            pltpu.VMEM((pad + tt, bw), F32),
            pltpu.VMEM((pad + tt, bw), F32),
        ],
        compiler_params=_params(("parallel", "parallel", "arbitrary")),
        name="rglru",
    )(xbg, xbg, conv_w, vec(conv_b), w_a, vec(b_a), w_i, vec(b_i), vec(lam))


def _recurrent_block(x, seq, batch, rec_norm, w_in, conv_w, conv_b, w_a, b_a, w_i, b_i, lam, w_o, *, tm):
    xbg = _norm_matmul(x, rec_norm, w_in, tm=tm, tn=512)
    y = _rglru(xbg, batch, seq, conv_w, conv_b, w_a, b_a, w_i, b_i, lam, tt=min(512, seq))
    return _matmul_res([y], w_o, x, tm=tm, tn=512)


def _router_kernel(x_ref, g_ref, wr_ref, idx_ref, wt_ref, cnt_ref, hp_ref, run_ref, *, rows):
    @pl.when(pl.program_id(0) == 0)
    def _():
        run_ref[...] = jnp.zeros_like(run_ref)

    hn = _rms(x_ref[...], g_ref[...])
    bits = lax.bitcast_convert_type(hn.astype(BF16).astype(F32), jnp.uint32)
    half = bits.shape[1] // 2
    words = jnp.bitwise_or(jnp.bitwise_and(bits[:, half:], jnp.uint32(0xFFFF0000)),
                           jnp.right_shift(bits[:, :half], jnp.uint32(16)))
    for c in range(hp_ref.shape[1]):
        hp_ref[:, c, :] = words[:, c * LANES:(c + 1) * LANES]
    logits = jnp.dot(hn, wr_ref[...], precision=lax.Precision.HIGHEST, preferred_element_type=F32)
    lane = lax.broadcasted_iota(jnp.int32, (rows, LANES), 1)
    lowest = jnp.float32(-3.0e38)
    logits = jnp.where(lane < N_EXPERTS, logits, lowest)
    m1 = jnp.max(logits, axis=1, keepdims=True)
    e1 = jnp.min(jnp.where(logits == m1, lane, LANES), axis=1, keepdims=True)
    rest = jnp.where(lane == e1, lowest, logits)
    m2 = jnp.max(rest, axis=1, keepdims=True)
    e2 = jnp.min(jnp.where(rest == m2, lane, LANES), axis=1, keepdims=True)
    ex = jnp.exp(m2 - m1)
    w1 = 1.0 / (1.0 + ex)
    w2 = ex / (1.0 + ex)

    onehot = jnp.where(jnp.logical_or(lane == e1, lane == e2), 1.0, 0.0)
    rr = lax.broadcasted_iota(jnp.int32, (rows, rows), 0)
    cc = lax.broadcasted_iota(jnp.int32, (rows, rows), 1)
    before = jnp.where(cc < rr, 1.0, 0.0).astype(BF16)
    rank = jnp.dot(before, onehot.astype(BF16), preferred_element_type=F32) + run_ref[0:1, :]
    r1 = jnp.sum(jnp.where(lane == e1, rank, 0.0), axis=1, keepdims=True).astype(jnp.int32)
    r2 = jnp.sum(jnp.where(lane == e2, rank, 0.0), axis=1, keepdims=True).astype(jnp.int32)
    run_ref[0:1, :] = run_ref[0:1, :] + jnp.sum(onehot, axis=0, keepdims=True)

    packed = jnp.where(lane == 0, e1, jnp.where(lane == 1, e2, jnp.where(lane == 2, r1, r2)))
    idx_ref[...] = packed[:, 0:idx_ref.shape[1]]
    wts = jnp.where(lane == 0, w1, w2)
    wt_ref[...] = wts[:, 0:wt_ref.shape[1]]
    cnt_ref[...] = run_ref[...].astype(jnp.int32)


def _router(x, g, router, *, rows):
    t, d = x.shape
    wr = jnp.pad(router, ((0, 0), (0, LANES - router.shape[1])))
    kern = functools.partial(_router_kernel, rows=rows)
    return pl.pallas_call(
        kern,
        out_shape=(
            jax.ShapeDtypeStruct((t, SUBLANES), jnp.int32),
            jax.ShapeDtypeStruct((t, SUBLANES), F32),
            jax.ShapeDtypeStruct((SUBLANES, LANES), jnp.int32),
            jax.ShapeDtypeStruct((t, d // 2 // LANES, LANES), jnp.uint32),
        ),
        grid=(t // rows,),
        in_specs=[
            pl.BlockSpec((rows, d), lambda i: (i, 0)),
            pl.BlockSpec((1, d), lambda i: (0, 0)),
            pl.BlockSpec((d, LANES), lambda i: (0, 0)),
        ],
        out_specs=(
            pl.BlockSpec((rows, SUBLANES), lambda i: (i, 0)),
            pl.BlockSpec((rows, SUBLANES), lambda i: (i, 0)),
            pl.BlockSpec((SUBLANES, LANES), lambda i: (0, 0)),
            pl.BlockSpec((rows, d // 2 // LANES, LANES), lambda i: (i, 0, 0)),
        ),
        scratch_shapes=[pltpu.VMEM((SUBLANES, LANES), F32)],
        compiler_params=_params(("arbitrary",)),
        name="moe_router",
    )(x, g.reshape(1, d), wr)


MOE_SUB_ROWS = 256


def _moe_ffn_kernel(te_ref, nu_ref, tv_ref, tok_ref, tokn_ref, hp_ref, wg_ref, wu_ref, wd_ref, o_ref,
                    xg_ref, hn_ref, sem, *, tm, n_f):
    del te_ref
    i = pl.program_id(0)
    f = pl.program_id(1)
    n_used = nu_ref[0]
    slot = lax.rem(i, 2)
    per_step = tm // n_f
    extra = tm - per_step * n_f

    def row_copy(tok, r, s):
        return pltpu.make_async_copy(hp_ref.at[tok[r]], xg_ref.at[s, r], sem.at[s])

    @pl.when(jnp.logical_and(i >= n_used, f == 0))
    def _():
        o_ref[...] = jnp.zeros_like(o_ref)

    @pl.when(i < n_used)
    def _():
        @pl.when(f == 0)
        def _():
            @pl.when(i == 0)
            def _():
                def start(r, carry):
                    row_copy(tok_ref, r, 0).start()
                    return carry
                lax.fori_loop(0, tm, start, 0)

            pltpu.make_async_copy(hp_ref.at[pl.ds(0, tm)], xg_ref.at[slot], sem.at[slot]).wait()
            chunks = pltpu.einshape("tcl->ctl", xg_ref[slot])
            half = hn_ref.shape[1] // 2
            for c in range(chunks.shape[0]):
                words = chunks[c]
                lo = lax.bitcast_convert_type(jnp.left_shift(words, jnp.uint32(16)), F32)
                hi = lax.bitcast_convert_type(jnp.bitwise_and(words, jnp.uint32(0xFFFF0000)), F32)
                hn_ref[:, c * LANES:(c + 1) * LANES] = lo.astype(BF16)
                hn_ref[:, half + c * LANES:half + (c + 1) * LANES] = hi.astype(BF16)
            o_ref[...] = jnp.zeros_like(o_ref)

        @pl.when(i + 1 < n_used)
        def _():
            base = f * per_step + jnp.minimum(f, extra)
            for rr in range(per_step):
                row_copy(tokn_ref, base + rr, 1 - slot).start()
            if extra:
                @pl.when(f < extra)
                def _():
                    row_copy(tokn_ref, base + per_step, 1 - slot).start()

        n_sub = lax.shift_right_logical(tv_ref[i] + (MOE_SUB_ROWS - 1), MOE_SUB_ROWS.bit_length() - 1)
        for k in range(1, tm // MOE_SUB_ROWS + 1):
            @pl.when(n_sub == k)
            def _(k=k):
                rows = k * MOE_SUB_ROWS
                hn = hn_ref[0:rows, :]
                gate = jnp.dot(hn, wg_ref[0].astype(BF16), preferred_element_type=F32)
                up = jnp.dot(hn, wu_ref[0].astype(BF16), preferred_element_type=F32)
                h = _silu_mul(gate, up).astype(BF16)
                o_ref[0:rows, :] += jnp.dot(h, wd_ref[0].astype(BF16), preferred_element_type=F32)


def _moe_ffn(hp, tok, we_gate, we_up, we_down, tile_expert, n_used, tile_valid, *, tm, tf):
    d = 2 * hp.shape[1] * hp.shape[2]
    n_rows = tok.shape[0]
    dff = we_gate.shape[2]
    n_f = dff // tf
    n_tiles = n_rows // tm
    assert tm % MOE_SUB_ROWS == 0 and MOE_SUB_ROWS & (MOE_SUB_ROWS - 1) == 0

    def tile(i, nu):
        return jnp.minimum(i, nu[0] - 1)

    def fidx(i, f, nu):
        return jnp.where(i < nu[0], f, n_f - 1)

    grid_spec = pltpu.PrefetchScalarGridSpec(
        num_scalar_prefetch=3,
        grid=(n_tiles, n_f),
        in_specs=[
            pl.BlockSpec((tm,), lambda i, f, te, nu, tv: (tile(i, nu),), memory_space=pltpu.SMEM),
            pl.BlockSpec((tm,), lambda i, f, te, nu, tv: (tile(i + 1, nu),), memory_space=pltpu.SMEM),
            pl.BlockSpec(memory_space=pl.ANY),
            pl.BlockSpec((1, d, tf), lambda i, f, te, nu, tv: (te[tile(i, nu)], 0, fidx(i, f, nu))),
            pl.BlockSpec((1, d, tf), lambda i, f, te, nu, tv: (te[tile(i, nu)], 0, fidx(i, f, nu))),
            pl.BlockSpec((1, tf, d), lambda i, f, te, nu, tv: (te[tile(i, nu)], fidx(i, f, nu), 0)),
        ],
        out_specs=pl.BlockSpec((tm, d), lambda i, f, te, nu, tv: (i, 0)),
        scratch_shapes=[
            pltpu.VMEM((2, tm) + hp.shape[1:], jnp.uint32),
            pltpu.VMEM((tm, d), BF16),
            pltpu.SemaphoreType.DMA((2,)),
        ],
    )
    return pl.pallas_call(
        functools.partial(_moe_ffn_kernel, tm=tm, n_f=n_f),
        out_shape=jax.ShapeDtypeStruct((n_rows, d), F32),
        grid_spec=grid_spec,
        compiler_params=_params(("arbitrary", "arbitrary")),
        name="moe_ffn",
    )(tile_expert, n_used, tile_valid, tok, tok, hp, we_gate, we_up, we_down)


COMBINE_ROWS = 512


def _combine_kernel(pos_ref, x_ref, wt_ref, ys_ref, o_ref, y1_ref, y2_ref, sem):
    def row_copy(r, k, dst):
        return pltpu.make_async_copy(ys_ref.at[pl.ds(pos_ref[2 * r + k], 1)], dst.at[pl.ds(r, 1)], sem)

    def start(r, carry):
        row_copy(r, 0, y1_ref).start()
        row_copy(r, 1, y2_ref).start()
        return carry

    lax.fori_loop(0, COMBINE_ROWS, start, 0)
    for dst in (y1_ref, y2_ref):
        pltpu.make_async_copy(ys_ref.at[pl.ds(0, COMBINE_ROWS)], dst, sem).wait()
    wt = wt_ref[...]
    o_ref[...] = x_ref[...] + wt[:, 0:1] * y1_ref[...] + wt[:, 1:2] * y2_ref[...]


def _combine(x, wts, ys, pos_flat):
    t, d = x.shape
    return pl.pallas_call(
        _combine_kernel,
        out_shape=jax.ShapeDtypeStruct((t, d), F32),
        grid=(t // COMBINE_ROWS,),
        in_specs=[
            pl.BlockSpec((2 * COMBINE_ROWS,), lambda i: (i,), memory_space=pltpu.SMEM),
            pl.BlockSpec((COMBINE_ROWS, d), lambda i: (i, 0)),
            pl.BlockSpec((COMBINE_ROWS, SUBLANES), lambda i: (i, 0)),
            pl.BlockSpec(memory_space=pl.ANY),
        ],
        out_specs=pl.BlockSpec((COMBINE_ROWS, d), lambda i: (i, 0)),
        scratch_shapes=[
            pltpu.VMEM((COMBINE_ROWS, d), F32),
            pltpu.VMEM((COMBINE_ROWS, d), F32),
            pltpu.SemaphoreType.DMA(()),
        ],
        compiler_params=_params(("arbitrary",)),
        name="moe_combine",
    )(pos_flat, x, wts, ys)


def _moe_block(x, ffn_norm, router, we_gate, we_up, we_down, *, tm=1024, tf=256):
    t, d = x.shape
    idx, wts, cnt, hp = _router(x, ffn_norm, router, rows=256)
    n_tiles = (TOP_K * t) // tm + N_EXPERTS
    counts = cnt[0, :N_EXPERTS]
    padded = ((counts + tm - 1) // tm) * tm
    ends = jnp.cumsum(padded)
    offsets = ends - padded
    pos = jnp.take(offsets, idx[:, 0:2]) + idx[:, 2:4]
    pos_flat = pos.reshape(-1).astype(jnp.int32)
    tile_start = jnp.arange(n_tiles, dtype=jnp.int32) * tm
    tile_expert = jnp.minimum(jnp.sum(tile_start[:, None] >= ends[None, :], axis=1), N_EXPERTS - 1).astype(jnp.int32)
    n_used = (ends[-1:] // tm).astype(jnp.int32)
    tile_valid = jnp.clip(jnp.take(offsets + counts, tile_expert) - tile_start, 0, tm).astype(jnp.int32)
    tok = jnp.zeros((n_tiles * tm,), jnp.int32).at[pos_flat].set(jnp.repeat(jnp.arange(t, dtype=jnp.int32), TOP_K))
    ys = _moe_ffn(hp, tok, we_gate, we_up, we_down, tile_expert, n_used, tile_valid, tm=tm, tf=tf)
    return _combine(x, wts, ys, pos_flat)


def kernel(x, ev_attn_norm, ev_w_in, ev_q_norm_sb, ev_k_norm_sb, ev_q_norm_moba, ev_k_norm_moba, ev_w_o, ev_ffn_norm, ev_w_gate, ev_w_up, ev_w_down, od_rec_norm, od_w_in, od_conv_w, od_conv_b, od_w_a, od_b_a, od_w_i, od_b_i, od_lambda, od_w_o, od_ffn_norm, od_router, od_we_gate, od_we_up, od_we_down):
    batch, seq, d = x.shape
    t = batch * seq
    tm = min(1024, seq)
    h = x.reshape(t, d)
    h = _attention_layer(h, seq, batch, ev_attn_norm[0], ev_w_in[0], ev_q_norm_sb[0], ev_k_norm_sb[0],
                         ev_q_norm_moba[0], ev_k_norm_moba[0], ev_w_o[0], ev_ffn_norm[0],
                         ev_w_gate[0], ev_w_up[0], ev_w_down[0], tm=tm)
    h = _recurrent_block(h, seq, batch, od_rec_norm[0], od_w_in[0], od_conv_w[0], od_conv_b[0], od_w_a[0],
                         od_b_a[0], od_w_i[0], od_b_i[0], od_lambda[0], od_w_o[0], tm=tm)
    h = _moe_block(h, od_ffn_norm[0], od_router[0], od_we_gate[0], od_we_up[0], od_we_down[0])
    return h.reshape(batch, seq, d)
```

```python
import functools
import math

import jax
import jax.numpy as jnp
from jax import lax
from jax.experimental import pallas as pl
from jax.experimental.pallas import tpu as pltpu

F32 = jnp.float32
BF16 = jnp.bfloat16

HEAD_DIM = 128
N_HEADS_SB = 8
N_HEADS_MOBA = 8
ROPE_DIM = HEAD_DIM // 4
ROPE_THETA = 500000.0
MOBA_BLOCK = 256
MOBA_TOPK = 3
LRU_BLOCKS = 8
LRU_C = 8.0
CONV_WIDTH = 4
N_EXPERTS = 8
TOP_K = 2
NORM_EPS = 1e-6
NEG_INF = -1e30

LANES = 128
SUBLANES = 8
VMEM_LIMIT_BYTES = 56 * 1024 * 1024


def _params(semantics, vmem=VMEM_LIMIT_BYTES):
    return pltpu.CompilerParams(dimension_semantics=semantics, vmem_limit_bytes=vmem)


def _rms(x, g):
    ms = jnp.mean(x * x, axis=-1, keepdims=True)
    return x * lax.rsqrt(ms + NORM_EPS) * g


def _norm_matmul_kernel(x_ref, g_ref, w_ref, o_ref, hn_ref):
    @pl.when(pl.program_id(1) == 0)
    def _():
        hn_ref[...] = _rms(x_ref[...], g_ref[...]).astype(BF16)

    y = jnp.dot(hn_ref[...], w_ref[...].astype(BF16), preferred_element_type=F32)
    o_ref[...] = y.astype(o_ref.dtype)


def _qkv_kernel(x_ref, g_ref, w_ref, hg_ref, c_ref, sa_ref, sb_ref, o_ref, hn_ref, *, tn, sec_w):
    j = pl.program_id(1)

    @pl.when(j == 0)
    def _():
        hn_ref[...] = _rms(x_ref[...], g_ref[...]).astype(BF16)

    y = jnp.dot(hn_ref[...], w_ref[...].astype(BF16), preferred_element_type=F32)
    sec = j // (sec_w // tn)
    is_v = jnp.logical_or(sec == 2, sec == 5)
    is_mb = jnp.logical_or(sec == 3, sec == 4)
    is_sb = jnp.logical_or(sec == 0, sec == 1)

    def split_dot(t, mat):
        hi = t.astype(BF16)
        lo = (t - hi.astype(F32)).astype(BF16)
        return (jnp.dot(hi, mat, preferred_element_type=F32) + jnp.dot(lo, mat, preferred_element_type=F32))

    def head_norm(hh):
        t = y[:, hh * HEAD_DIM:(hh + 1) * HEAD_DIM]
        mean_mat = jnp.full((HEAD_DIM, HEAD_DIM), 1.0 / HEAD_DIM, BF16)
        ms = jnp.dot((t * t).astype(BF16), mean_mat, preferred_element_type=F32)
        return t * lax.rsqrt(ms + NORM_EPS) * hg_ref[0]

    @pl.when(is_v)
    def _():
        o_ref[...] = y.astype(o_ref.dtype)

    @pl.when(is_sb)
    def _():
        for hh in range(tn // HEAD_DIM):
            o_ref[:, hh * HEAD_DIM:(hh + 1) * HEAD_DIM] = head_norm(hh).astype(o_ref.dtype)

    @pl.when(is_mb)
    def _():
        half = ROPE_DIM // 2
        src = lax.broadcasted_iota(jnp.int32, (HEAD_DIM, 2 * HEAD_DIM), 0)
        dst = lax.broadcasted_iota(jnp.int32, (HEAD_DIM, 2 * HEAD_DIM), 1)
        from_below = jnp.logical_and(dst < HEAD_DIM, src + half == dst)
        from_above = jnp.logical_and(dst >= HEAD_DIM, src - half == dst - HEAD_DIM)
        shift_mat = jnp.where(jnp.logical_or(from_below, from_above), 1.0, 0.0).astype(BF16)
        for hh in range(tn // HEAD_DIM):
            t = head_norm(hh)
            rot = split_dot(t, shift_mat)
            t = t * c_ref[0] + rot[:, :HEAD_DIM] * sa_ref[0] + rot[:, HEAD_DIM:] * sb_ref[0]
            o_ref[:, hh * HEAD_DIM:(hh + 1) * HEAD_DIM] = t.astype(o_ref.dtype)


def _norm_matmul(x, g, w, *, tm, tn, out_dtype=BF16):
    t, d = x.shape
    n = w.shape[1]
    return pl.pallas_call(
        _norm_matmul_kernel,
        out_shape=jax.ShapeDtypeStruct((t, n), out_dtype),
        grid=(t // tm, n // tn),
        in_specs=[
            pl.BlockSpec((tm, d), lambda i, j: (i, 0)),
            pl.BlockSpec((1, d), lambda i, j: (0, 0)),
            pl.BlockSpec((d, tn), lambda i, j: (0, j)),
        ],
        out_specs=pl.BlockSpec((tm, tn), lambda i, j: (i, j)),
        scratch_shapes=[pltpu.VMEM((tm, d), BF16)],
        compiler_params=_params(("parallel", "arbitrary")),
        name="norm_matmul",
    )(x, g.reshape(1, d), w)


def _rope_tables(seq):
    half = ROPE_DIM // 2
    inv_freq = ROPE_THETA ** (-jnp.arange(0, ROPE_DIM, 2, dtype=F32) / ROPE_DIM)
    ang = jnp.arange(seq, dtype=F32)[:, None] * inv_freq[None, :]
    cos, sin = jnp.cos(ang), jnp.sin(ang)
    ones = jnp.ones((seq, HEAD_DIM - ROPE_DIM), F32)
    zeros = jnp.zeros((seq, HEAD_DIM - ROPE_DIM), F32)
    zh = jnp.zeros((seq, half), F32)
    c = jnp.concatenate([cos, cos, ones], axis=1)
    sa = jnp.concatenate([zh, sin, zeros], axis=1)
    sb = jnp.concatenate([-sin, zh, zeros], axis=1)
    ident = jnp.ones((seq, HEAD_DIM), F32)
    z = jnp.zeros((seq, HEAD_DIM), F32)
    return jnp.stack([ident, c]), jnp.stack([z, sa]), jnp.stack([z, sb])


def _qkv_proj(x, g, w, head_gains, seq, *, tm, tn):
    t, d = x.shape
    n = w.shape[1]
    sec_w = n // 6
    c, sa, sb = _rope_tables(seq)
    n_s = seq // tm
    kern = functools.partial(_qkv_kernel, tn=tn, sec_w=sec_w)
    per_sec = sec_w // tn
    return pl.pallas_call(
        kern,
        out_shape=jax.ShapeDtypeStruct((t, n), BF16),
        grid=(t // tm, n // tn),
        in_specs=[
            pl.BlockSpec((tm, d), lambda i, j: (i, 0)),
            pl.BlockSpec((1, d), lambda i, j: (0, 0)),
            pl.BlockSpec((d, tn), lambda i, j: (0, j)),
            pl.BlockSpec((1, 1, HEAD_DIM), lambda i, j: (j // per_sec, 0, 0)),
            pl.BlockSpec((1, tm, HEAD_DIM), lambda i, j: ((j // per_sec) // 3, i % n_s, 0)),
            pl.BlockSpec((1, tm, HEAD_DIM), lambda i, j: ((j // per_sec) // 3, i % n_s, 0)),
            pl.BlockSpec((1, tm, HEAD_DIM), lambda i, j: ((j // per_sec) // 3, i % n_s, 0)),
        ],
        out_specs=pl.BlockSpec((tm, tn), lambda i, j: (i, j)),
        scratch_shapes=[pltpu.VMEM((tm, d), BF16)],
        compiler_params=_params(("parallel", "arbitrary")),
        name="qkv_proj",
    )(x, g.reshape(1, d), w, head_gains, c, sa, sb)


def _matmul_res_kernel(*refs, n_a):
    a_refs = refs[:n_a]
    w_ref, r_ref, o_ref, wb_ref = refs[n_a:]

    @pl.when(pl.program_id(0) == 0)
    def _():
        wb_ref[...] = w_ref[...].astype(BF16)

    acc = r_ref[...]
    k0 = 0
    for a_ref in a_refs:
        k = a_ref.shape[1]
        acc = acc + jnp.dot(a_ref[...], wb_ref[k0:k0 + k, :], preferred_element_type=F32)
        k0 += k
    o_ref[...] = acc


def _matmul_res(a_list, w, res, *, tm):
    t = res.shape[0]
    k, n = w.shape
    kern = functools.partial(_matmul_res_kernel, n_a=len(a_list))
    a_specs = [pl.BlockSpec((tm, a.shape[1]), lambda i: (i, 0)) for a in a_list]
    return pl.pallas_call(
        kern,
        out_shape=jax.ShapeDtypeStruct((t, n), F32),
        grid=(t // tm,),
        in_specs=a_specs + [
            pl.BlockSpec((k, n), lambda i: (0, 0), pipeline_mode=pl.Buffered(1)),
            pl.BlockSpec((tm, n), lambda i: (i, 0)),
        ],
        out_specs=pl.BlockSpec((tm, n), lambda i: (i, 0)),
        scratch_shapes=[pltpu.VMEM((k, n), BF16)],
        compiler_params=_params(("arbitrary",)),
        name="matmul_res",
    )(*a_list, w, res)


def _silu_mul(gate, up):
    return gate * (1.0 / (1.0 + jnp.exp(-gate))) * up


def _ffn_kernel(x_ref, g_ref, wg_ref, wu_ref, wd_ref, o_ref, hn_ref):
    @pl.when(pl.program_id(1) == 0)
    def _():
        x = x_ref[...]
        hn_ref[...] = _rms(x, g_ref[...]).astype(BF16)
        o_ref[...] = x

    hn = hn_ref[...]
    gate = jnp.dot(hn, wg_ref[...].astype(BF16), preferred_element_type=F32)
    up = jnp.dot(hn, wu_ref[...].astype(BF16), preferred_element_type=F32)
    h = _silu_mul(gate, up).astype(BF16)
    o_ref[...] += jnp.dot(h, wd_ref[...].astype(BF16), preferred_element_type=F32)


def _ffn(x, g, wg, wu, wd, *, tm, tf):
    t, d = x.shape
    dff = wg.shape[1]
    return pl.pallas_call(
        _ffn_kernel,
        out_shape=jax.ShapeDtypeStruct((t, d), F32),
        grid=(t // tm, dff // tf),
        in_specs=[
            pl.BlockSpec((tm, d), lambda i, f: (i, 0), pipeline_mode=pl.Buffered(1)),
            pl.BlockSpec((1, d), lambda i, f: (0, 0)),
            pl.BlockSpec((d, tf), lambda i, f: (0, f)),
            pl.BlockSpec((d, tf), lambda i, f: (0, f)),
            pl.BlockSpec((tf, d), lambda i, f: (f, 0)),
        ],
        out_specs=pl.BlockSpec((tm, d), lambda i, f: (i, 0)),
        scratch_shapes=[pltpu.VMEM((tm, d), BF16)],
        compiler_params=_params(("parallel", "arbitrary")),
        name="ffn_swiglu",
    )(x, g.reshape(1, d), wg, wu, wd)


def _transpose_v(v_ref, vt_ref, blk):
    for c in range(v_ref.shape[0] // blk):
        vt_ref[:, c * blk:(c + 1) * blk] = v_ref[c * blk:(c + 1) * blk, :].astype(F32).T.astype(vt_ref.dtype)


SB_DEAD_LOG_WEIGHT = -110.0


def _sb_kernel(q_ref, k_ref, v_ref, o_ref, vt_ref, acc_ref, *, blk):
    qi = pl.program_id(2)

    @pl.when(qi == 0)
    def _():
        _transpose_v(v_ref, vt_ref, blk)

    q = q_ref[...]
    key = lax.broadcasted_iota(jnp.int32, (blk, blk), 0)
    qry = lax.broadcasted_iota(jnp.int32, (blk, blk), 1)
    later_mat = jnp.where(qry > key, 1.0, 0.0).astype(BF16)

    def log_weights(kj, diagonal):
        start = pl.multiple_of(kj * blk, blk)
        z = lax.dot_general(k_ref[pl.ds(start, blk), :], q, (((1,), (1,)), ((), ())),
                            preferred_element_type=F32)
        sp = jnp.log(1.0 + jnp.exp(-jnp.abs(z)))
        log_beta = jnp.minimum(z, 0.0) - sp
        log_keep = log_beta - z
        if diagonal:
            past = key < qry
            log_keep = jnp.where(past, log_keep, 0.0)
        lk_hi = log_keep.astype(BF16)
        lk_lo = (log_keep - lk_hi.astype(F32)).astype(BF16)
        later = (jnp.dot(later_mat, lk_hi, preferred_element_type=F32)
                 + jnp.dot(later_mat, lk_lo, preferred_element_type=F32))
        lw = log_beta + later
        if diagonal:
            lw = jnp.where(past, lw, NEG_INF)
        return start, lw, jnp.sum(log_keep, axis=0, keepdims=True)

    def weighted_values(start, lw, c):
        w = jnp.exp(lw + c).astype(BF16)
        return jnp.dot(vt_ref[:, pl.ds(start, blk)], w, preferred_element_type=F32)

    has_prev = qi > 0
    st0, lw0, sum0 = log_weights(qi, True)
    st1, lw1, sum1 = log_weights(jnp.maximum(qi - 1, 0), False)
    acc_ref[...] = (weighted_values(st0, lw0, jnp.zeros((1, blk), F32))
                    + weighted_values(st1, lw1, jnp.where(has_prev, sum0, NEG_INF)))
    c = sum0 + jnp.where(has_prev, sum1, 0.0)

    def cond(carry):
        kj, _, alive = carry
        return jnp.logical_and(kj >= 0, alive)

    def body(carry):
        kj, c, _ = carry
        st, lw, sm = log_weights(kj, False)
        acc_ref[...] += weighted_values(st, lw, c)
        c = c + sm
        return kj - 1, c, jnp.max(c) > SB_DEAD_LOG_WEIGHT

    lax.while_loop(cond, body, (qi - 2, c, jnp.max(c) > SB_DEAD_LOG_WEIGHT))
    o_ref[...] = acc_ref[...].T.astype(o_ref.dtype)


def _sb_attention(qkv, batch, seq, *, blk):
    t = qkv.shape[0]
    bq = blk
    nq = seq // bq
    kern = functools.partial(_sb_kernel, blk=blk)
    return pl.pallas_call(
        kern,
        out_shape=jax.ShapeDtypeStruct((t, N_HEADS_SB * HEAD_DIM), BF16),
        grid=(batch, N_HEADS_SB, nq),
        in_specs=[
            pl.BlockSpec((bq, HEAD_DIM), lambda b, h, i: (b * nq + i, h)),
            pl.BlockSpec((seq, HEAD_DIM), lambda b, h, i: (b, N_HEADS_SB + h)),
            pl.BlockSpec((seq, HEAD_DIM), lambda b, h, i: (b, 2 * N_HEADS_SB + h)),
        ],
        out_specs=pl.BlockSpec((bq, HEAD_DIM), lambda b, h, i: (b * nq + i, h)),
        scratch_shapes=[pltpu.VMEM((HEAD_DIM, seq), BF16), pltpu.VMEM((HEAD_DIM, bq), F32)],
        compiler_params=_params(("parallel", "parallel", "arbitrary")),
        name="sb_attention",
    )(qkv, qkv, qkv)


MOBA_GROUP = 4
MOBA_HEADS_PER_STEP = 1


def _moba_kernel(q_ref, k_ref, v_ref, o_ref, vt_ref, kmean_ref, *, blk, seq, heads):
    qi = pl.program_id(2)
    n_blocks = seq // blk
    hd = HEAD_DIM

    @pl.when(qi == 0)
    def _():
        r = lax.broadcasted_iota(jnp.int32, (n_blocks, seq), 0)
        s = lax.broadcasted_iota(jnp.int32, (n_blocks, seq), 1)
        avg = jnp.where(s // blk == r, 1.0 / blk, 0.0).astype(BF16)
        for hh in range(heads):
            for c in range(n_blocks):
                vt_ref[hh, :, c * blk:(c + 1) * blk] = (
                    v_ref[c * blk:(c + 1) * blk, hh * hd:(hh + 1) * hd].astype(F32).T.astype(vt_ref.dtype))
            kmean_ref[hh] = jnp.dot(avg, k_ref[:, hh * hd:(hh + 1) * hd], preferred_element_type=F32)

    blk_id = lax.broadcasted_iota(jnp.int32, (n_blocks, blk), 0)
    key = lax.broadcasted_iota(jnp.int32, (blk, blk), 0)
    qry = lax.broadcasted_iota(jnp.int32, (blk, blk), 1)
    lowest = jnp.float32(-3.0e38)
    is_past = blk_id < qi
    own = pl.multiple_of(qi * blk, blk)

    def scores(hh, q, rows):
        return lax.dot_general(k_ref[rows, hh * hd:(hh + 1) * hd], q, (((1,), (1,)), ((), ())),
                               preferred_element_type=F32)

    state = []
    for hh in range(heads):
        q = q_ref[:, hh * hd:(hh + 1) * hd]
        gate = lax.dot_general(kmean_ref[hh], q.astype(F32), (((1,), (1,)), ((), ())),
                               precision=lax.Precision.HIGHEST, preferred_element_type=F32)
        g = jnp.where(is_past, gate, lowest)
        bias = jnp.full((n_blocks, blk), NEG_INF, F32)
        for _ in range(MOBA_TOPK):
            mx = jnp.max(g, axis=0, keepdims=True)
            first = jnp.min(jnp.where(g == mx, blk_id, n_blocks), axis=0, keepdims=True)
            pick = blk_id == first
            bias = jnp.where(jnp.logical_and(pick, is_past), 0.0, bias)
            g = jnp.where(pick, lowest, g)

        s_own = jnp.where(key <= qry, scores(hh, q, pl.ds(own, blk)), NEG_INF)
        m0 = jnp.max(s_own, axis=0, keepdims=True)
        p0 = jnp.exp2(s_own - m0)
        l0 = jnp.sum(p0, axis=0, keepdims=True)
        acc0 = jnp.dot(vt_ref[hh, :, pl.ds(own, blk)], p0.astype(BF16), preferred_element_type=F32)
        state.append((q, bias, m0, l0, acc0))

    def past_blocks(extent):
        for hh in range(heads):
            q, bias, m0, l0, acc0 = state[hh]
            s = scores(hh, q, slice(0, extent * blk))
            s = jnp.concatenate([s[i * blk:(i + 1) * blk, :] + bias[i:i + 1, :] for i in range(extent)], axis=0)
            m = jnp.maximum(m0, jnp.max(s, axis=0, keepdims=True))
            alpha = jnp.exp2(m0 - m)
            p = jnp.exp2(s - m)
            l = alpha * l0 + jnp.sum(p, axis=0, keepdims=True)
            acc = alpha * acc0 + jnp.dot(vt_ref[hh, :, 0:extent * blk], p.astype(BF16),
                                         preferred_element_type=F32)
            o_ref[:, hh * hd:(hh + 1) * hd] = (acc / l).T.astype(o_ref.dtype)

    n_groups = -(-n_blocks // MOBA_GROUP)
    for grp in range(n_groups):
        @pl.when(qi // MOBA_GROUP == grp)
        def _(grp=grp):
            past_blocks(min((grp + 1) * MOBA_GROUP, n_blocks))


def _moba_attention(qkv, batch, seq, *, heads=MOBA_HEADS_PER_STEP):
    t = qkv.shape[0]
    blk = MOBA_BLOCK
    nq = seq // blk
    width = heads * HEAD_DIM
    col0 = 3 * N_HEADS_SB // heads
    nh = N_HEADS_MOBA // heads
    kern = functools.partial(_moba_kernel, blk=blk, seq=seq, heads=heads)
    return pl.pallas_call(
        kern,
        out_shape=jax.ShapeDtypeStruct((t, N_HEADS_MOBA * HEAD_DIM), BF16),
        grid=(batch, nh, nq),
        in_specs=[
            pl.BlockSpec((blk, width), lambda b, h, i: (b * nq + i, col0 + h)),
            pl.BlockSpec((seq, width), lambda b, h, i: (b, col0 + nh + h)),
            pl.BlockSpec((seq, width), lambda b, h, i: (b, col0 + 2 * nh + h)),
        ],
        out_specs=pl.BlockSpec((blk, width), lambda b, h, i: (b * nq + i, h)),
        scratch_shapes=[
            pltpu.VMEM((heads, HEAD_DIM, seq), BF16),
            pltpu.VMEM((heads, seq // blk, HEAD_DIM), F32),
        ],
        compiler_params=_params(("parallel", "parallel", "arbitrary")),
        name="moba_attention",
    )(qkv, qkv, qkv)


def _attention_layer(x, seq, batch, attn_norm, w_in, qn_sb, kn_sb, qn_mb, kn_mb, w_o,
                     ffn_norm, w_gate, w_up, w_down, *, tm):
    scale = HEAD_DIM ** -0.5
    ones = jnp.ones((HEAD_DIM,), F32)
    head_gains = jnp.stack([qn_sb * scale, kn_sb, ones, qn_mb * (scale * math.log2(math.e)), kn_mb, ones])
    head_gains = head_gains.reshape(6, 1, HEAD_DIM)
    qkv = _qkv_proj(x, attn_norm, w_in, head_gains, seq, tm=tm, tn=512)
    o_sb = _sb_attention(qkv, batch, seq, blk=256)
    o_mb = _moba_attention(qkv, batch, seq)
    x = _matmul_res([o_sb, o_mb], w_o, x, tm=tm // 2)
    return _ffn(x, ffn_norm, w_gate, w_up, w_down, tm=tm, tf=256)


def _sigmoid(x):
    return 1.0 / (1.0 + jnp.exp(-x))


def _gelu_tanh(x):
    return 0.5 * x * (1.0 + jnp.tanh(math.sqrt(2.0 / math.pi) * (x + 0.044715 * (x * x * x))))


def _rglru_kernel(xb_ref, gt_ref, cw_ref, cb_ref, wa_ref, ba_ref, wi_ref, bi_ref, lam_ref, o_ref,
                  xpad_ref, h_ref, a_ref, b_ref, *, tt, pad):
    ti = pl.program_id(2)

    @pl.when(ti == 0)
    def _():
        xpad_ref[0:SUBLANES, :] = jnp.zeros((SUBLANES, xpad_ref.shape[1]), F32)
        h_ref[...] = jnp.zeros_like(h_ref)
        a_ref[0:pad, :] = jnp.ones((pad, a_ref.shape[1]), F32)
        b_ref[0:pad, :] = jnp.zeros((pad, b_ref.shape[1]), F32)

    xb = xb_ref[...].astype(F32)
    xpad_ref[SUBLANES:SUBLANES + tt, :] = xb
    cw = cw_ref[...]
    xc = cb_ref[...] + xb * cw[CONV_WIDTH - 1:CONV_WIDTH, :]
    for tap in range(CONV_WIDTH - 1):
        off = SUBLANES - (CONV_WIDTH - 1) + tap
        xc = xc + xpad_ref[off:off + tt, :] * cw[tap:tap + 1, :]
    xpad_ref[0:SUBLANES, :] = xpad_ref[tt:tt + SUBLANES, :]

    xcb = xc.astype(BF16)
    r = _sigmoid(jnp.dot(xcb, wa_ref[0].astype(BF16), preferred_element_type=F32) + ba_ref[...])
    ig = _sigmoid(jnp.dot(xcb, wi_ref[0].astype(BF16), preferred_element_type=F32) + bi_ref[...])
    neg_lam = -lam_ref[...]
    sp = jnp.maximum(neg_lam, 0.0) + jnp.log1p(jnp.exp(-jnp.abs(neg_lam)))
    log_a = (-LRU_C * r) * sp
    a = jnp.exp(log_a)
    u = jnp.sqrt(-jnp.tanh(log_a) * (a * a + 1.0)) * (ig * xc)

    a_ref[pad:pad + tt, :] = a
    b_ref[pad:pad + tt, :] = u
    s = 1
    while s < tt:
        a_cur = a_ref[pad:pad + tt, :]
        b_cur = b_ref[pad:pad + tt, :]
        a_sh = a_ref[pad - s:pad - s + tt, :]
        b_sh = b_ref[pad - s:pad - s + tt, :]
        b_ref[pad:pad + tt, :] = a_cur * b_sh + b_cur
        a_ref[pad:pad + tt, :] = a_cur * a_sh
        s *= 2
    h = a_ref[pad:pad + tt, :] * h_ref[0:1, :] + b_ref[pad:pad + tt, :]
    h_ref[0:1, :] = h[tt - 1:tt, :]
    o_ref[...] = (h * _gelu_tanh(gt_ref[...].astype(F32))).astype(o_ref.dtype)


def _rglru(xbg, batch, seq, conv_w, conv_b, w_a, b_a, w_i, b_i, lam, *, tt):
    t = xbg.shape[0]
    width = xbg.shape[1] // 2
    bw = width // LRU_BLOCKS
    n_t = seq // tt
    pad = tt // 2
    kern = functools.partial(_rglru_kernel, tt=tt, pad=pad)
    vec = lambda a: a.reshape(1, width)
    vspec = pl.BlockSpec((1, bw), lambda b, c, i: (0, c))
    return pl.pallas_call(
        kern,
        out_shape=jax.ShapeDtypeStruct((t, width), BF16),
        grid=(batch, LRU_BLOCKS, n_t),
        in_specs=[
            pl.BlockSpec((tt, bw), lambda b, c, i: (b * n_t + i, c)),
            pl.BlockSpec((tt, bw), lambda b, c, i: (b * n_t + i, LRU_BLOCKS + c)),
            pl.BlockSpec((CONV_WIDTH, bw), lambda b, c, i: (0, c)),
            vspec,
            pl.BlockSpec((1, bw, bw), lambda b, c, i: (c, 0, 0)),
            vspec,
            pl.BlockSpec((1, bw, bw), lambda b, c, i: (c, 0, 0)),
            vspec,
            vspec,
        ],
        out_specs=pl.BlockSpec((tt, bw), lambda b, c, i: (b * n_t + i, c)),
        scratch_shapes=[
            pltpu.VMEM((tt + 2 * SUBLANES, bw), F32),
            pltpu.VMEM((SUBLANES, bw), F32),
            pltpu.VMEM((pad + tt, bw), F32),
            pltpu.VMEM((pad + tt, bw), F32),
        ],
        compiler_params=_params(("parallel", "parallel", "arbitrary")),
        name="rglru",
    )(xbg, xbg, conv_w, vec(conv_b), w_a, vec(b_a), w_i, vec(b_i), vec(lam))


def _recurrent_block(x, seq, batch, rec_norm, w_in, conv_w, conv_b, w_a, b_a, w_i, b_i, lam, w_o, *, tm):
    xbg = _norm_matmul(x, rec_norm, w_in, tm=tm, tn=512)
    y = _rglru(xbg, batch, seq, conv_w, conv_b, w_a, b_a, w_i, b_i, lam, tt=min(512, seq))
    return _matmul_res([y], w_o, x, tm=tm // 2)


def _router_kernel(x_ref, g_ref, wr_ref, idx_ref, wt_ref, cnt_ref, hp_ref, run_ref, *, rows):
    @pl.when(pl.program_id(0) == 0)
    def _():
        run_ref[...] = jnp.zeros_like(run_ref)

    hn = _rms(x_ref[...], g_ref[...])
    bits = lax.bitcast_convert_type(hn.astype(BF16).astype(F32), jnp.uint32)
    half = bits.shape[1] // 2
    words = jnp.bitwise_or(jnp.bitwise_and(bits[:, half:], jnp.uint32(0xFFFF0000)),
                           jnp.right_shift(bits[:, :half], jnp.uint32(16)))
    for c in range(hp_ref.shape[1]):
        hp_ref[:, c, :] = words[:, c * LANES:(c + 1) * LANES]
    logits = jnp.dot(hn, wr_ref[...], precision=lax.Precision.HIGHEST, preferred_element_type=F32)
    lane = lax.broadcasted_iota(jnp.int32, (rows, LANES), 1)
    lowest = jnp.float32(-3.0e38)
    logits = jnp.where(lane < N_EXPERTS, logits, lowest)
    m1 = jnp.max(logits, axis=1, keepdims=True)
    e1 = jnp.min(jnp.where(logits == m1, lane, LANES), axis=1, keepdims=True)
    rest = jnp.where(lane == e1, lowest, logits)
    m2 = jnp.max(rest, axis=1, keepdims=True)
    e2 = jnp.min(jnp.where(rest == m2, lane, LANES), axis=1, keepdims=True)
    ex = jnp.exp(m2 - m1)
    w1 = 1.0 / (1.0 + ex)
    w2 = ex / (1.0 + ex)

    onehot = jnp.where(jnp.logical_or(lane == e1, lane == e2), 1.0, 0.0)
    rr = lax.broadcasted_iota(jnp.int32, (rows, rows), 0)
    cc = lax.broadcasted_iota(jnp.int32, (rows, rows), 1)
    before = jnp.where(cc < rr, 1.0, 0.0).astype(BF16)
    rank = jnp.dot(before, onehot.astype(BF16), preferred_element_type=F32) + run_ref[0:1, :]
    r1 = jnp.sum(jnp.where(lane == e1, rank, 0.0), axis=1, keepdims=True).astype(jnp.int32)
    r2 = jnp.sum(jnp.where(lane == e2, rank, 0.0), axis=1, keepdims=True).astype(jnp.int32)
    run_ref[0:1, :] = run_ref[0:1, :] + jnp.sum(onehot, axis=0, keepdims=True)

    packed = jnp.where(lane == 0, e1, jnp.where(lane == 1, e2, jnp.where(lane == 2, r1, r2)))
    idx_ref[...] = packed[:, 0:idx_ref.shape[1]]
    wts = jnp.where(lane == 0, w1, w2)
    wt_ref[...] = wts[:, 0:wt_ref.shape[1]]
    cnt_ref[...] = run_ref[...].astype(jnp.int32)


def _router(x, g, router, *, rows):
    t, d = x.shape
    wr = jnp.pad(router, ((0, 0), (0, LANES - router.shape[1])))
    kern = functools.partial(_router_kernel, rows=rows)
    return pl.pallas_call(
        kern,
        out_shape=(
            jax.ShapeDtypeStruct((t, SUBLANES), jnp.int32),
            jax.ShapeDtypeStruct((t, SUBLANES), F32),
            jax.ShapeDtypeStruct((SUBLANES, LANES), jnp.int32),
            jax.ShapeDtypeStruct((t, d // 2 // LANES, LANES), jnp.uint32),
        ),
        grid=(t // rows,),
        in_specs=[
            pl.BlockSpec((rows, d), lambda i: (i, 0)),
            pl.BlockSpec((1, d), lambda i: (0, 0)),
            pl.BlockSpec((d, LANES), lambda i: (0, 0)),
        ],
        out_specs=(
            pl.BlockSpec((rows, SUBLANES), lambda i: (i, 0)),
            pl.BlockSpec((rows, SUBLANES), lambda i: (i, 0)),
            pl.BlockSpec((SUBLANES, LANES), lambda i: (0, 0)),
            pl.BlockSpec((rows, d // 2 // LANES, LANES), lambda i: (i, 0, 0)),
        ),
        scratch_shapes=[pltpu.VMEM((SUBLANES, LANES), F32)],
        compiler_params=_params(("arbitrary",)),
        name="moe_router",
    )(x, g.reshape(1, d), wr)


MOE_SUB_ROWS = 256


def _moe_ffn_kernel(te_ref, nu_ref, tv_ref, tok_ref, tokn_ref, hp_ref, wg_ref, wu_ref, wd_ref, o_ref,
                    xg_ref, hn_ref, sem, *, tm, n_f):
    del te_ref
    i = pl.program_id(0)
    f = pl.program_id(1)
    n_used = nu_ref[0]
    slot = lax.rem(i, 2)
    per_step = tm // n_f
    extra = tm - per_step * n_f

    def row_copy(tok, r, s):
        return pltpu.make_async_copy(hp_ref.at[tok[r]], xg_ref.at[s, r], sem.at[s])

    @pl.when(jnp.logical_and(i >= n_used, f == 0))
    def _():
        o_ref[...] = jnp.zeros_like(o_ref)

    @pl.when(i < n_used)
    def _():
        @pl.when(f == 0)
        def _():
            @pl.when(i == 0)
            def _():
                def start(r, carry):
                    row_copy(tok_ref, r, 0).start()
                    return carry
                lax.fori_loop(0, tm, start, 0)

            pltpu.make_async_copy(hp_ref.at[pl.ds(0, tm)], xg_ref.at[slot], sem.at[slot]).wait()
            chunks = pltpu.einshape("tcl->ctl", xg_ref[slot])
            half = hn_ref.shape[1] // 2
            for c in range(chunks.shape[0]):
                words = chunks[c]
                lo = lax.bitcast_convert_type(jnp.left_shift(words, jnp.uint32(16)), F32)
                hi = lax.bitcast_convert_type(jnp.bitwise_and(words, jnp.uint32(0xFFFF0000)), F32)
                hn_ref[:, c * LANES:(c + 1) * LANES] = lo.astype(BF16)
                hn_ref[:, half + c * LANES:half + (c + 1) * LANES] = hi.astype(BF16)
            o_ref[...] = jnp.zeros_like(o_ref)

        @pl.when(i + 1 < n_used)
        def _():
            base = f * per_step + jnp.minimum(f, extra)
            for rr in range(per_step):
                row_copy(tokn_ref, base + rr, 1 - slot).start()
            if extra:
                @pl.when(f < extra)
                def _():
                    row_copy(tokn_ref, base + per_step, 1 - slot).start()

        n_sub = lax.shift_right_logical(tv_ref[i] + (MOE_SUB_ROWS - 1), MOE_SUB_ROWS.bit_length() - 1)
        for k in range(1, tm // MOE_SUB_ROWS + 1):
            @pl.when(n_sub == k)
            def _(k=k):
                rows = k * MOE_SUB_ROWS
                hn = hn_ref[0:rows, :]
                gate = jnp.dot(hn, wg_ref[0].astype(BF16), preferred_element_type=F32)
                up = jnp.dot(hn, wu_ref[0].astype(BF16), preferred_element_type=F32)
                h = _silu_mul(gate, up).astype(BF16)
                o_ref[0:rows, :] += jnp.dot(h, wd_ref[0].astype(BF16), preferred_element_type=F32)


def _moe_ffn(hp, tok, we_gate, we_up, we_down, tile_expert, n_used, tile_valid, *, tm, tf):
    d = 2 * hp.shape[1] * hp.shape[2]
    n_rows = tok.shape[0]
    dff = we_gate.shape[2]
    n_f = dff // tf
    n_tiles = n_rows // tm
    assert tm % MOE_SUB_ROWS == 0 and MOE_SUB_ROWS & (MOE_SUB_ROWS - 1) == 0

    def tile(i, nu):
        return jnp.minimum(i, nu[0] - 1)

    def fidx(i, f, nu):
        return jnp.where(i < nu[0], f, n_f - 1)

    grid_spec = pltpu.PrefetchScalarGridSpec(
        num_scalar_prefetch=3,
        grid=(n_tiles, n_f),
        in_specs=[
            pl.BlockSpec((tm,), lambda i, f, te, nu, tv: (tile(i, nu),), memory_space=pltpu.SMEM),
            pl.BlockSpec((tm,), lambda i, f, te, nu, tv: (tile(i + 1, nu),), memory_space=pltpu.SMEM),
            pl.BlockSpec(memory_space=pl.ANY),
            pl.BlockSpec((1, d, tf), lambda i, f, te, nu, tv: (te[tile(i, nu)], 0, fidx(i, f, nu))),
            pl.BlockSpec((1, d, tf), lambda i, f, te, nu, tv: (te[tile(i, nu)], 0, fidx(i, f, nu))),
            pl.BlockSpec((1, tf, d), lambda i, f, te, nu, tv: (te[tile(i, nu)], fidx(i, f, nu), 0)),
        ],
        out_specs=pl.BlockSpec((tm, d), lambda i, f, te, nu, tv: (i, 0)),
        scratch_shapes=[
            pltpu.VMEM((2, tm) + hp.shape[1:], jnp.uint32),
            pltpu.VMEM((tm, d), BF16),
            pltpu.SemaphoreType.DMA((2,)),
        ],
    )
    return pl.pallas_call(
        functools.partial(_moe_ffn_kernel, tm=tm, n_f=n_f),
        out_shape=jax.ShapeDtypeStruct((n_rows, d), F32),
        grid_spec=grid_spec,
        compiler_params=_params(("arbitrary", "arbitrary")),
        name="moe_ffn",
    )(tile_expert, n_used, tile_valid, tok, tok, hp, we_gate, we_up, we_down)


COMBINE_ROWS = 512


def _combine_kernel(pos_ref, x_ref, wt_ref, ys_ref, o_ref, y1_ref, y2_ref, sem):
    def row_copy(r, k, dst):
        return pltpu.make_async_copy(ys_ref.at[pl.ds(pos_ref[2 * r + k], 1)], dst.at[pl.ds(r, 1)], sem)

    def start(r, carry):
        row_copy(r, 0, y1_ref).start()
        row_copy(r, 1, y2_ref).start()
        return carry

    lax.fori_loop(0, COMBINE_ROWS, start, 0)
    for dst in (y1_ref, y2_ref):
        pltpu.make_async_copy(ys_ref.at[pl.ds(0, COMBINE_ROWS)], dst, sem).wait()
    wt = wt_ref[...]
    o_ref[...] = x_ref[...] + wt[:, 0:1] * y1_ref[...] + wt[:, 1:2] * y2_ref[...]


def _combine(x, wts, ys, pos_flat):
    t, d = x.shape
    return pl.pallas_call(
        _combine_kernel,
        out_shape=jax.ShapeDtypeStruct((t, d), F32),
        grid=(t // COMBINE_ROWS,),
        in_specs=[
            pl.BlockSpec((2 * COMBINE_ROWS,), lambda i: (i,), memory_space=pltpu.SMEM),
            pl.BlockSpec((COMBINE_ROWS, d), lambda i: (i, 0)),
            pl.BlockSpec((COMBINE_ROWS, SUBLANES), lambda i: (i, 0)),
            pl.BlockSpec(memory_space=pl.ANY),
        ],
        out_specs=pl.BlockSpec((COMBINE_ROWS, d), lambda i: (i, 0)),
        scratch_shapes=[
            pltpu.VMEM((COMBINE_ROWS, d), F32),
            pltpu.VMEM((COMBINE_ROWS, d), F32),
            pltpu.SemaphoreType.DMA(()),
        ],
        compiler_params=_params(("arbitrary",)),
        name="moe_combine",
    )(pos_flat, x, wts, ys)


def _moe_block(x, ffn_norm, router, we_gate, we_up, we_down, *, tm=1024, tf=256):
    t, d = x.shape
    idx, wts, cnt, hp = _router(x, ffn_norm, router, rows=256)
    n_tiles = (TOP_K * t) // tm + N_EXPERTS
    counts = cnt[0, :N_EXPERTS]
    padded = ((counts + tm - 1) // tm) * tm
    ends = jnp.cumsum(padded)
    offsets = ends - padded
    pos = jnp.take(offsets, idx[:, 0:2]) + idx[:, 2:4]
    pos_flat = pos.reshape(-1).astype(jnp.int32)
    tile_start = jnp.arange(n_tiles, dtype=jnp.int32) * tm
    tile_expert = jnp.minimum(jnp.sum(tile_start[:, None] >= ends[None, :], axis=1), N_EXPERTS - 1).astype(jnp.int32)
    n_used = (ends[-1:] // tm).astype(jnp.int32)
    tile_valid = jnp.clip(jnp.take(offsets + counts, tile_expert) - tile_start, 0, tm).astype(jnp.int32)
    tok = jnp.zeros((n_tiles * tm,), jnp.int32).at[pos_flat].set(jnp.repeat(jnp.arange(t, dtype=jnp.int32), TOP_K))
    ys = _moe_ffn(hp, tok, we_gate, we_up, we_down, tile_expert, n_used, tile_valid, tm=tm, tf=tf)
    return _combine(x, wts, ys, pos_flat)


def kernel(x, ev_attn_norm, ev_w_in, ev_q_norm_sb, ev_k_norm_sb, ev_q_norm_moba, ev_k_norm_moba, ev_w_o, ev_ffn_norm, ev_w_gate, ev_w_up, ev_w_down, od_rec_norm, od_w_in, od_conv_w, od_conv_b, od_w_a, od_b_a, od_w_i, od_b_i, od_lambda, od_w_o, od_ffn_norm, od_router, od_we_gate, od_we_up, od_we_down):
    batch, seq, d = x.shape
    t = batch * seq
    tm = min(1024, seq)
    h = x.reshape(t, d)
    h = _attention_layer(h, seq, batch, ev_attn_norm[0], ev_w_in[0], ev_q_norm_sb[0], ev_k_norm_sb[0],
                         ev_q_norm_moba[0], ev_k_norm_moba[0], ev_w_o[0], ev_ffn_norm[0],
                         ev_w_gate[0], ev_w_up[0], ev_w_down[0], tm=tm)
    h = _recurrent_block(h, seq, batch, od_rec_norm[0], od_w_in[0], od_conv_w[0], od_conv_b[0], od_w_a[0],
                         od_b_a[0], od_w_i[0], od_b_i[0], od_lambda[0], od_w_o[0], tm=tm)
    h = _moe_block(h, od_ffn_norm[0], od_router[0], od_we_gate[0], od_we_up[0], od_we_down[0])
    return h.reshape(batch, seq, d)
```

```python
import functools
import math

import jax
import jax.numpy as jnp
from jax import lax
from jax.experimental import pallas as pl
from jax.experimental.pallas import tpu as pltpu

F32 = jnp.float32
BF16 = jnp.bfloat16

HEAD_DIM = 128
N_HEADS_SB = 8
N_HEADS_MOBA = 8
ROPE_DIM = HEAD_DIM // 4
ROPE_THETA = 500000.0
MOBA_BLOCK = 256
MOBA_TOPK = 3
LRU_BLOCKS = 8
LRU_C = 8.0
CONV_WIDTH = 4
N_EXPERTS = 8
TOP_K = 2
NORM_EPS = 1e-6
NEG_INF = -1e30

LANES = 128
SUBLANES = 8
VMEM_LIMIT_BYTES = 56 * 1024 * 1024


def _params(semantics, vmem=VMEM_LIMIT_BYTES):
    return pltpu.CompilerParams(dimension_semantics=semantics, vmem_limit_bytes=vmem)


def _rms(x, g):
    ms = jnp.mean(x * x, axis=-1, keepdims=True)
    return x * lax.rsqrt(ms + NORM_EPS) * g


def _norm_matmul_kernel(x_ref, g_ref, w_ref, o_ref, hn_ref):
    @pl.when(pl.program_id(1) == 0)
    def _():
        hn_ref[...] = _rms(x_ref[...], g_ref[...]).astype(BF16)

    y = jnp.dot(hn_ref[...], w_ref[...].astype(BF16), preferred_element_type=F32)
    o_ref[...] = y.astype(o_ref.dtype)


def _qkv_kernel(x_ref, g_ref, w_ref, hg_ref, c_ref, sa_ref, sb_ref, o_ref, hn_ref, *, tn, sec_w):
    j = pl.program_id(1)

    @pl.when(j == 0)
    def _():
        hn_ref[...] = _rms(x_ref[...], g_ref[...]).astype(BF16)

    y = jnp.dot(hn_ref[...], w_ref[...].astype(BF16), preferred_element_type=F32)
    sec = j // (sec_w // tn)
    is_v = jnp.logical_or(sec == 2, sec == 5)
    is_mb = jnp.logical_or(sec == 3, sec == 4)
    is_sb = jnp.logical_or(sec == 0, sec == 1)

    def split_dot(t, mat):
        hi = t.astype(BF16)
        lo = (t - hi.astype(F32)).astype(BF16)
        return (jnp.dot(hi, mat, preferred_element_type=F32) + jnp.dot(lo, mat, preferred_element_type=F32))

    def head_norm(hh):
        t = y[:, hh * HEAD_DIM:(hh + 1) * HEAD_DIM]
        mean_mat = jnp.full((HEAD_DIM, HEAD_DIM), 1.0 / HEAD_DIM, BF16)
        ms = jnp.dot((t * t).astype(BF16), mean_mat, preferred_element_type=F32)
        return t * lax.rsqrt(ms + NORM_EPS) * hg_ref[0]

    @pl.when(is_v)
    def _():
        o_ref[...] = y.astype(o_ref.dtype)

    @pl.when(is_sb)
    def _():
        for hh in range(tn // HEAD_DIM):
            o_ref[:, hh * HEAD_DIM:(hh + 1) * HEAD_DIM] = head_norm(hh).astype(o_ref.dtype)

    @pl.when(is_mb)
    def _():
        half = ROPE_DIM // 2
        src = lax.broadcasted_iota(jnp.int32, (HEAD_DIM, 2 * HEAD_DIM), 0)
        dst = lax.broadcasted_iota(jnp.int32, (HEAD_DIM, 2 * HEAD_DIM), 1)
        from_below = jnp.logical_and(dst < HEAD_DIM, src + half == dst)
        from_above = jnp.logical_and(dst >= HEAD_DIM, src - half == dst - HEAD_DIM)
        shift_mat = jnp.where(jnp.logical_or(from_below, from_above), 1.0, 0.0).astype(BF16)
        for hh in range(tn // HEAD_DIM):
            t = head_norm(hh)
            rot = split_dot(t, shift_mat)
            t = t * c_ref[0] + rot[:, :HEAD_DIM] * sa_ref[0] + rot[:, HEAD_DIM:] * sb_ref[0]
            o_ref[:, hh * HEAD_DIM:(hh + 1) * HEAD_DIM] = t.astype(o_ref.dtype)


def _norm_matmul(x, g, w, *, tm, tn, out_dtype=BF16):
    t, d = x.shape
    n = w.shape[1]
    return pl.pallas_call(
        _norm_matmul_kernel,
        out_shape=jax.ShapeDtypeStruct((t, n), out_dtype),
        grid=(t // tm, n // tn),
        in_specs=[
            pl.BlockSpec((tm, d), lambda i, j: (i, 0)),
            pl.BlockSpec((1, d), lambda i, j: (0, 0)),
            pl.BlockSpec((d, tn), lambda i, j: (0, j)),
        ],
        out_specs=pl.BlockSpec((tm, tn), lambda i, j: (i, j)),
        scratch_shapes=[pltpu.VMEM((tm, d), BF16)],
        compiler_params=_params(("parallel", "arbitrary")),
        name="norm_matmul",
    )(x, g.reshape(1, d), w)


def _rope_tables(seq):
    half = ROPE_DIM // 2
    inv_freq = ROPE_THETA ** (-jnp.arange(0, ROPE_DIM, 2, dtype=F32) / ROPE_DIM)
    ang = jnp.arange(seq, dtype=F32)[:, None] * inv_freq[None, :]
    cos, sin = jnp.cos(ang), jnp.sin(ang)
    ones = jnp.ones((seq, HEAD_DIM - ROPE_DIM), F32)
    zeros = jnp.zeros((seq, HEAD_DIM - ROPE_DIM), F32)
    zh = jnp.zeros((seq, half), F32)
    c = jnp.concatenate([cos, cos, ones], axis=1)
    sa = jnp.concatenate([zh, sin, zeros], axis=1)
    sb = jnp.concatenate([-sin, zh, zeros], axis=1)
    ident = jnp.ones((seq, HEAD_DIM), F32)
    z = jnp.zeros((seq, HEAD_DIM), F32)
    return jnp.stack([ident, c]), jnp.stack([z, sa]), jnp.stack([z, sb])


def _qkv_proj(x, g, w, head_gains, seq, *, tm, tn):
    t, d = x.shape
    n = w.shape[1]
    sec_w = n // 6
    c, sa, sb = _rope_tables(seq)
    n_s = seq // tm
    kern = functools.partial(_qkv_kernel, tn=tn, sec_w=sec_w)
    per_sec = sec_w // tn
    return pl.pallas_call(
        kern,
        out_shape=jax.ShapeDtypeStruct((t, n), BF16),
        grid=(t // tm, n // tn),
        in_specs=[
            pl.BlockSpec((tm, d), lambda i, j: (i, 0)),
            pl.BlockSpec((1, d), lambda i, j: (0, 0)),
            pl.BlockSpec((d, tn), lambda i, j: (0, j)),
            pl.BlockSpec((1, 1, HEAD_DIM), lambda i, j: (j // per_sec, 0, 0)),
            pl.BlockSpec((1, tm, HEAD_DIM), lambda i, j: ((j // per_sec) // 3, i % n_s, 0)),
            pl.BlockSpec((1, tm, HEAD_DIM), lambda i, j: ((j // per_sec) // 3, i % n_s, 0)),
            pl.BlockSpec((1, tm, HEAD_DIM), lambda i, j: ((j // per_sec) // 3, i % n_s, 0)),
        ],
        out_specs=pl.BlockSpec((tm, tn), lambda i, j: (i, j)),
        scratch_shapes=[pltpu.VMEM((tm, d), BF16)],
        compiler_params=_params(("parallel", "arbitrary")),
        name="qkv_proj",
    )(x, g.reshape(1, d), w, head_gains, c, sa, sb)


def _matmul_res_kernel(*refs, n_a):
    a_refs = refs[:n_a]
    w_ref, r_ref, o_ref, wb_ref = refs[n_a:]

    @pl.when(pl.program_id(0) == 0)
    def _():
        wb_ref[...] = w_ref[...].astype(BF16)

    acc = r_ref[...]
    k0 = 0
    for a_ref in a_refs:
        k = a_ref.shape[1]
        acc = acc + jnp.dot(a_ref[...], wb_ref[k0:k0 + k, :], preferred_element_type=F32)
        k0 += k
    o_ref[...] = acc


def _matmul_res(a_list, w, res, *, tm):
    t = res.shape[0]
    k, n = w.shape
    kern = functools.partial(_matmul_res_kernel, n_a=len(a_list))
    a_specs = [pl.BlockSpec((tm, a.shape[1]), lambda i: (i, 0)) for a in a_list]
    return pl.pallas_call(
        kern,
        out_shape=jax.ShapeDtypeStruct((t, n), F32),
        grid=(t // tm,),
        in_specs=a_specs + [
            pl.BlockSpec((k, n), lambda i: (0, 0), pipeline_mode=pl.Buffered(1)),
            pl.BlockSpec((tm, n), lambda i: (i, 0)),
        ],
        out_specs=pl.BlockSpec((tm, n), lambda i: (i, 0)),
        scratch_shapes=[pltpu.VMEM((k, n), BF16)],
        compiler_params=_params(("arbitrary",)),
        name="matmul_res",
    )(*a_list, w, res)


def _silu_mul(gate, up):
    return gate * (1.0 / (1.0 + jnp.exp(-gate))) * up


def _ffn_kernel(x_ref, g_ref, wg_ref, wu_ref, wd_ref, o_ref, hn_ref):
    @pl.when(pl.program_id(1) == 0)
    def _():
        x = x_ref[...]
        hn_ref[...] = _rms(x, g_ref[...]).astype(BF16)
        o_ref[...] = x

    hn = hn_ref[...]
    gate = jnp.dot(hn, wg_ref[...].astype(BF16), preferred_element_type=F32)
    up = jnp.dot(hn, wu_ref[...].astype(BF16), preferred_element_type=F32)
    h = _silu_mul(gate, up).astype(BF16)
    o_ref[...] += jnp.dot(h, wd_ref[...].astype(BF16), preferred_element_type=F32)


def _ffn(x, g, wg, wu, wd, *, tm, tf):
    t, d = x.shape
    dff = wg.shape[1]
    return pl.pallas_call(
        _ffn_kernel,
        out_shape=jax.ShapeDtypeStruct((t, d), F32),
        grid=(t // tm, dff // tf),
        in_specs=[
            pl.BlockSpec((tm, d), lambda i, f: (i, 0), pipeline_mode=pl.Buffered(1)),
            pl.BlockSpec((1, d), lambda i, f: (0, 0)),
            pl.BlockSpec((d, tf), lambda i, f: (0, f)),
            pl.BlockSpec((d, tf), lambda i, f: (0, f)),
            pl.BlockSpec((tf, d), lambda i, f: (f, 0)),
        ],
        out_specs=pl.BlockSpec((tm, d), lambda i, f: (i, 0)),
        scratch_shapes=[pltpu.VMEM((tm, d), BF16)],
        compiler_params=_params(("parallel", "arbitrary")),
        name="ffn_swiglu",
    )(x, g.reshape(1, d), wg, wu, wd)


def _transpose_v(v_ref, vt_ref, blk):
    for c in range(v_ref.shape[0] // blk):
        vt_ref[:, c * blk:(c + 1) * blk] = v_ref[c * blk:(c + 1) * blk, :].astype(F32).T.astype(vt_ref.dtype)


SB_DEAD_LOG_WEIGHT = -110.0


def _sb_kernel(q_ref, k_ref, v_ref, o_ref, vt_ref, acc_ref, *, blk):
    qi = pl.program_id(2)

    @pl.when(qi == 0)
    def _():
        _transpose_v(v_ref, vt_ref, blk)

    q = q_ref[...]
    key = lax.broadcasted_iota(jnp.int32, (blk, blk), 0)
    qry = lax.broadcasted_iota(jnp.int32, (blk, blk), 1)
    later_mat = jnp.where(qry > key, 1.0, 0.0).astype(BF16)

    def log_weights(kj, diagonal):
        start = pl.multiple_of(kj * blk, blk)
        z = lax.dot_general(k_ref[pl.ds(start, blk), :], q, (((1,), (1,)), ((), ())),
                            preferred_element_type=F32)
        sp = jnp.log(1.0 + jnp.exp(-jnp.abs(z)))
        log_beta = jnp.minimum(z, 0.0) - sp
        log_keep = log_beta - z
        if diagonal:
            past = key < qry
            log_keep = jnp.where(past, log_keep, 0.0)
        lk_hi = log_keep.astype(BF16)
        lk_lo = (log_keep - lk_hi.astype(F32)).astype(BF16)
        later = (jnp.dot(later_mat, lk_hi, preferred_element_type=F32)
                 + jnp.dot(later_mat, lk_lo, preferred_element_type=F32))
        lw = log_beta + later
        if diagonal:
            lw = jnp.where(past, lw, NEG_INF)
        return start, lw, jnp.sum(log_keep, axis=0, keepdims=True)

    def weighted_values(start, lw, c):
        w = jnp.exp(lw + c).astype(BF16)
        return jnp.dot(vt_ref[:, pl.ds(start, blk)], w, preferred_element_type=F32)

    has_prev = qi > 0
    st0, lw0, sum0 = log_weights(qi, True)
    st1, lw1, sum1 = log_weights(jnp.maximum(qi - 1, 0), False)
    acc_ref[...] = (weighted_values(st0, lw0, jnp.zeros((1, blk), F32))
                    + weighted_values(st1, lw1, jnp.where(has_prev, sum0, NEG_INF)))
    c = sum0 + jnp.where(has_prev, sum1, 0.0)

    def cond(carry):
        kj, _, alive = carry
        return jnp.logical_and(kj >= 0, alive)

    def body(carry):
        kj, c, _ = carry
        st, lw, sm = log_weights(kj, False)
        acc_ref[...] += weighted_values(st, lw, c)
        c = c + sm
        return kj - 1, c, jnp.max(c) > SB_DEAD_LOG_WEIGHT

    lax.while_loop(cond, body, (qi - 2, c, jnp.max(c) > SB_DEAD_LOG_WEIGHT))
    o_ref[...] = acc_ref[...].T.astype(o_ref.dtype)


def _sb_attention(qkv, batch, seq, *, blk):
    t = qkv.shape[0]
    bq = blk
    nq = seq // bq
    kern = functools.partial(_sb_kernel, blk=blk)
    return pl.pallas_call(
        kern,
        out_shape=jax.ShapeDtypeStruct((t, N_HEADS_SB * HEAD_DIM), BF16),
        grid=(batch, N_HEADS_SB, nq),
        in_specs=[
            pl.BlockSpec((bq, HEAD_DIM), lambda b, h, i: (b * nq + i, h)),
            pl.BlockSpec((seq, HEAD_DIM), lambda b, h, i: (b, N_HEADS_SB + h)),
            pl.BlockSpec((seq, HEAD_DIM), lambda b, h, i: (b, 2 * N_HEADS_SB + h)),
        ],
        out_specs=pl.BlockSpec((bq, HEAD_DIM), lambda b, h, i: (b * nq + i, h)),
        scratch_shapes=[pltpu.VMEM((HEAD_DIM, seq), BF16), pltpu.VMEM((HEAD_DIM, bq), F32)],
        compiler_params=_params(("parallel", "parallel", "arbitrary")),
        name="sb_attention",
    )(qkv, qkv, qkv)


MOBA_GROUP = 2
MOBA_CHUNK = 16
MOBA_HEADS_PER_STEP = 1


def _moba_kernel(q_ref, k_ref, v_ref, o_ref, vt_ref, kmean_ref, *, blk, seq, heads):
    qi = pl.program_id(2)
    n_blocks = seq // blk
    hd = HEAD_DIM

    @pl.when(qi == 0)
    def _():
        r = lax.broadcasted_iota(jnp.int32, (n_blocks, seq), 0)
        s = lax.broadcasted_iota(jnp.int32, (n_blocks, seq), 1)
        avg = jnp.where(s // blk == r, 1.0 / blk, 0.0).astype(BF16)
        for hh in range(heads):
            for c in range(n_blocks):
                vt_ref[hh, :, c * blk:(c + 1) * blk] = (
                    v_ref[c * blk:(c + 1) * blk, hh * hd:(hh + 1) * hd].astype(F32).T.astype(vt_ref.dtype))
            kmean_ref[hh] = jnp.dot(avg, k_ref[:, hh * hd:(hh + 1) * hd], preferred_element_type=F32)

    blk_id = lax.broadcasted_iota(jnp.int32, (n_blocks, blk), 0)
    key = lax.broadcasted_iota(jnp.int32, (blk, blk), 0)
    qry = lax.broadcasted_iota(jnp.int32, (blk, blk), 1)
    lowest = jnp.float32(-3.0e38)
    is_past = blk_id < qi
    own = pl.multiple_of(qi * blk, blk)

    def scores(hh, q, rows):
        return lax.dot_general(k_ref[rows, hh * hd:(hh + 1) * hd], q, (((1,), (1,)), ((), ())),
                               preferred_element_type=F32)

    state = []
    for hh in range(heads):
        q = q_ref[:, hh * hd:(hh + 1) * hd]
        gate = lax.dot_general(kmean_ref[hh], q.astype(F32), (((1,), (1,)), ((), ())),
                               precision=lax.Precision.HIGHEST, preferred_element_type=F32)
        g = jnp.where(is_past, gate, lowest)
        bias = jnp.full((n_blocks, blk), NEG_INF, F32)
        for _ in range(MOBA_TOPK):
            mx = jnp.max(g, axis=0, keepdims=True)
            first = jnp.min(jnp.where(g == mx, blk_id, n_blocks), axis=0, keepdims=True)
            pick = blk_id == first
            bias = jnp.where(jnp.logical_and(pick, is_past), 0.0, bias)
            g = jnp.where(pick, lowest, g)

        s_own = jnp.where(key <= qry, scores(hh, q, pl.ds(own, blk)), NEG_INF)
        m0 = jnp.max(s_own, axis=0, keepdims=True)
        p0 = jnp.exp2(s_own - m0)
        l0 = jnp.sum(p0, axis=0, keepdims=True)
        acc0 = jnp.dot(vt_ref[hh, :, pl.ds(own, blk)], p0.astype(BF16), preferred_element_type=F32)
        state.append((q, bias, m0, l0, acc0))

    def past_blocks(extent):
        for hh in range(heads):
            q, bias, m, l, acc = state[hh]
            step = MOBA_CHUNK * blk

            def chunk_scores(c0):
                nb = min(MOBA_CHUNK, extent - c0)
                s = scores(hh, q, slice(c0 * blk, (c0 + nb) * blk))
                return jnp.concatenate([s[i * blk:(i + 1) * blk, :] + bias[c0 + i:c0 + i + 1, :]
                                        for i in range(nb)], axis=0)

            s_next = chunk_scores(0)
            for c0 in range(0, extent, MOBA_CHUNK):
                s = s_next
                if c0 + MOBA_CHUNK < extent:
                    s_next = chunk_scores(c0 + MOBA_CHUNK)
                m_new = jnp.maximum(m, jnp.max(s, axis=0, keepdims=True))
                alpha = jnp.exp2(m - m_new)
                p = jnp.exp2(s - m_new)
                l = alpha * l + jnp.sum(p, axis=0, keepdims=True)
                acc = alpha * acc + jnp.dot(vt_ref[hh, :, c0 * blk:c0 * blk + s.shape[0]], p.astype(BF16),
                                            preferred_element_type=F32)
                m = m_new
            del step
            o_ref[:, hh * hd:(hh + 1) * hd] = (acc / l).T.astype(o_ref.dtype)

    n_groups = -(-n_blocks // MOBA_GROUP)
    for grp in range(n_groups):
        @pl.when(qi // MOBA_GROUP == grp)
        def _(grp=grp):
            past_blocks(min((grp + 1) * MOBA_GROUP, n_blocks))


def _moba_attention(qkv, batch, seq, *, heads=MOBA_HEADS_PER_STEP):
    t = qkv.shape[0]
    blk = MOBA_BLOCK
    nq = seq // blk
    width = heads * HEAD_DIM
    col0 = 3 * N_HEADS_SB // heads
    nh = N_HEADS_MOBA // heads
    kern = functools.partial(_moba_kernel, blk=blk, seq=seq, heads=heads)
    return pl.pallas_call(
        kern,
        out_shape=jax.ShapeDtypeStruct((t, N_HEADS_MOBA * HEAD_DIM), BF16),
        grid=(batch, nh, nq),
        in_specs=[
            pl.BlockSpec((blk, width), lambda b, h, i: (b * nq + i, col0 + h)),
            pl.BlockSpec((seq, width), lambda b, h, i: (b, col0 + nh + h)),
            pl.BlockSpec((seq, width), lambda b, h, i: (b, col0 + 2 * nh + h)),
        ],
        out_specs=pl.BlockSpec((blk, width), lambda b, h, i: (b * nq + i, h)),
        scratch_shapes=[
            pltpu.VMEM((heads, HEAD_DIM, seq), BF16),
            pltpu.VMEM((heads, seq // blk, HEAD_DIM), F32),
        ],
        compiler_params=_params(("parallel", "parallel", "arbitrary")),
        name="moba_attention",
    )(qkv, qkv, qkv)


def _attention_layer(x, seq, batch, attn_norm, w_in, qn_sb, kn_sb, qn_mb, kn_mb, w_o,
                     ffn_norm, w_gate, w_up, w_down, *, tm):
    scale = HEAD_DIM ** -0.5
    ones = jnp.ones((HEAD_DIM,), F32)
    head_gains = jnp.stack([qn_sb * scale, kn_sb, ones, qn_mb * (scale * math.log2(math.e)), kn_mb, ones])
    head_gains = head_gains.reshape(6, 1, HEAD_DIM)
    qkv = _qkv_proj(x, attn_norm, w_in, head_gains, seq, tm=tm, tn=512)
    o_sb = _sb_attention(qkv, batch, seq, blk=256)
    o_mb = _moba_attention(qkv, batch, seq)
    x = _matmul_res([o_sb, o_mb], w_o, x, tm=tm // 2)
    return _ffn(x, ffn_norm, w_gate, w_up, w_down, tm=tm, tf=256)


def _sigmoid(x):
    return 1.0 / (1.0 + jnp.exp(-x))


def _gelu_tanh(x):
    return 0.5 * x * (1.0 + jnp.tanh(math.sqrt(2.0 / math.pi) * (x + 0.044715 * (x * x * x))))


def _rglru_kernel(xb_ref, gt_ref, cw_ref, cb_ref, wa_ref, ba_ref, wi_ref, bi_ref, lam_ref, o_ref,
                  xpad_ref, h_ref, a_ref, b_ref, *, tt, pad):
    ti = pl.program_id(2)

    @pl.when(ti == 0)
    def _():
        xpad_ref[0:SUBLANES, :] = jnp.zeros((SUBLANES, xpad_ref.shape[1]), F32)
        h_ref[...] = jnp.zeros_like(h_ref)
        a_ref[0:pad, :] = jnp.ones((pad, a_ref.shape[1]), F32)
        b_ref[0:pad, :] = jnp.zeros((pad, b_ref.shape[1]), F32)

    xb = xb_ref[...].astype(F32)
    xpad_ref[SUBLANES:SUBLANES + tt, :] = xb
    cw = cw_ref[...]
    xc = cb_ref[...] + xb * cw[CONV_WIDTH - 1:CONV_WIDTH, :]
    for tap in range(CONV_WIDTH - 1):
        off = SUBLANES - (CONV_WIDTH - 1) + tap
        xc = xc + xpad_ref[off:off + tt, :] * cw[tap:tap + 1, :]
    xpad_ref[0:SUBLANES, :] = xpad_ref[tt:tt + SUBLANES, :]

    xcb = xc.astype(BF16)
    r = _sigmoid(jnp.dot(xcb, wa_ref[0].astype(BF16), preferred_element_type=F32) + ba_ref[...])
    ig = _sigmoid(jnp.dot(xcb, wi_ref[0].astype(BF16), preferred_element_type=F32) + bi_ref[...])
    neg_lam = -lam_ref[...]
    sp = jnp.maximum(neg_lam, 0.0) + jnp.log1p(jnp.exp(-jnp.abs(neg_lam)))
    log_a = (-LRU_C * r) * sp
    a = jnp.exp(log_a)
    u = jnp.sqrt(-jnp.tanh(log_a) * (a * a + 1.0)) * (ig * xc)

    a_ref[pad:pad + tt, :] = a
    b_ref[pad:pad + tt, :] = u
    s = 1
    while s < tt:
        a_cur = a_ref[pad:pad + tt, :]
        b_cur = b_ref[pad:pad + tt, :]
        a_sh = a_ref[pad - s:pad - s + tt, :]
        b_sh = b_ref[pad - s:pad - s + tt, :]
        b_ref[pad:pad + tt, :] = a_cur * b_sh + b_cur
        a_ref[pad:pad + tt, :] = a_cur * a_sh
        s *= 2
    h = a_ref[pad:pad + tt, :] * h_ref[0:1, :] + b_ref[pad:pad + tt, :]
    h_ref[0:1, :] = h[tt - 1:tt, :]
    o_ref[...] = (h * _gelu_tanh(gt_ref[...].astype(F32))).astype(o_ref.dtype)


def _rglru(xbg, batch, seq, conv_w, conv_b, w_a, b_a, w_i, b_i, lam, *, tt):
    t = xbg.shape[0]
    width = xbg.shape[1] // 2
    bw = width // LRU_BLOCKS
    n_t = seq // tt
    pad = tt // 2
    kern = functools.partial(_rglru_kernel, tt=tt, pad=pad)
    vec = lambda a: a.reshape(1, width)
    vspec = pl.BlockSpec((1, bw), lambda b, c, i: (0, c))
    return pl.pallas_call(
        kern,
        out_shape=jax.ShapeDtypeStruct((t, width), BF16),
        grid=(batch, LRU_BLOCKS, n_t),
        in_specs=[
            pl.BlockSpec((tt, bw), lambda b, c, i: (b * n_t + i, c)),
            pl.BlockSpec((tt, bw), lambda b, c, i: (b * n_t + i, LRU_BLOCKS + c)),
            pl.BlockSpec((CONV_WIDTH, bw), lambda b, c, i: (0, c)),
            vspec,
            pl.BlockSpec((1, bw, bw), lambda b, c, i: (c, 0, 0)),
            vspec,
            pl.BlockSpec((1, bw, bw), lambda b, c, i: (c, 0, 0)),
            vspec,
            vspec,
        ],
        out_specs=pl.BlockSpec((tt, bw), lambda b, c, i: (b * n_t + i, c)),
        scratch_shapes=[
            pltpu.VMEM((tt + 2 * SUBLANES, bw), F32),
            pltpu.VMEM((SUBLANES, bw), F32),
            pltpu.VMEM((pad + tt, bw), F32),
            pltpu.VMEM((pad + tt, bw), F32),
        ],
        compiler_params=_params(("parallel", "parallel", "arbitrary")),
        name="rglru",
    )(xbg, xbg, conv_w, vec(conv_b), w_a, vec(b_a), w_i, vec(b_i), vec(lam))


def _recurrent_block(x, seq, batch, rec_norm, w_in, conv_w, conv_b, w_a, b_a, w_i, b_i, lam, w_o, *, tm):
    xbg = _norm_matmul(x, rec_norm, w_in, tm=tm, tn=512)
    y = _rglru(xbg, batch, seq, conv_w, conv_b, w_a, b_a, w_i, b_i, lam, tt=min(512, seq))
    return _matmul_res([y], w_o, x, tm=tm // 2)


def _router_kernel(x_ref, g_ref, wr_ref, idx_ref, wt_ref, cnt_ref, hp_ref, run_ref, *, rows):
    @pl.when(pl.program_id(0) == 0)
    def _():
        run_ref[...] = jnp.zeros_like(run_ref)

    hn = _rms(x_ref[...], g_ref[...])
    bits = lax.bitcast_convert_type(hn.astype(BF16).astype(F32), jnp.uint32)
    half = bits.shape[1] // 2
    words = jnp.bitwise_or(jnp.bitwise_and(bits[:, half:], jnp.uint32(0xFFFF0000)),
                           jnp.right_shift(bits[:, :half], jnp.uint32(16)))
    for c in range(hp_ref.shape[1]):
        hp_ref[:, c, :] = words[:, c * LANES:(c + 1) * LANES]
    logits = jnp.dot(hn, wr_ref[...], precision=lax.Precision.HIGHEST, preferred_element_type=F32)
    lane = lax.broadcasted_iota(jnp.int32, (rows, LANES), 1)
    lowest = jnp.float32(-3.0e38)
    logits = jnp.where(lane < N_EXPERTS, logits, lowest)
    m1 = jnp.max(logits, axis=1, keepdims=True)
    e1 = jnp.min(jnp.where(logits == m1, lane, LANES), axis=1, keepdims=True)
    rest = jnp.where(lane == e1, lowest, logits)
    m2 = jnp.max(rest, axis=1, keepdims=True)
    e2 = jnp.min(jnp.where(rest == m2, lane, LANES), axis=1, keepdims=True)
    ex = jnp.exp(m2 - m1)
    w1 = 1.0 / (1.0 + ex)
    w2 = ex / (1.0 + ex)

    onehot = jnp.where(jnp.logical_or(lane == e1, lane == e2), 1.0, 0.0)
    rr = lax.broadcasted_iota(jnp.int32, (rows, rows), 0)
    cc = lax.broadcasted_iota(jnp.int32, (rows, rows), 1)
    before = jnp.where(cc < rr, 1.0, 0.0).astype(BF16)
    rank = jnp.dot(before, onehot.astype(BF16), preferred_element_type=F32) + run_ref[0:1, :]
    r1 = jnp.sum(jnp.where(lane == e1, rank, 0.0), axis=1, keepdims=True).astype(jnp.int32)
    r2 = jnp.sum(jnp.where(lane == e2, rank, 0.0), axis=1, keepdims=True).astype(jnp.int32)
    run_ref[0:1, :] = run_ref[0:1, :] + jnp.sum(onehot, axis=0, keepdims=True)

    packed = jnp.where(lane == 0, e1, jnp.where(lane == 1, e2, jnp.where(lane == 2, r1, r2)))
    idx_ref[...] = packed[:, 0:idx_ref.shape[1]]
    wts = jnp.where(lane == 0, w1, w2)
    wt_ref[...] = wts[:, 0:wt_ref.shape[1]]
    cnt_ref[...] = run_ref[...].astype(jnp.int32)


def _router(x, g, router, *, rows):
    t, d = x.shape
    wr = jnp.pad(router, ((0, 0), (0, LANES - router.shape[1])))
    kern = functools.partial(_router_kernel, rows=rows)
    return pl.pallas_call(
        kern,
        out_shape=(
            jax.ShapeDtypeStruct((t, SUBLANES), jnp.int32),
            jax.ShapeDtypeStruct((t, SUBLANES), F32),
            jax.ShapeDtypeStruct((SUBLANES, LANES), jnp.int32),
            jax.ShapeDtypeStruct((t, d // 2 // LANES, LANES), jnp.uint32),
        ),
        grid=(t // rows,),
        in_specs=[
            pl.BlockSpec((rows, d), lambda i: (i, 0)),
            pl.BlockSpec((1, d), lambda i: (0, 0)),
            pl.BlockSpec((d, LANES), lambda i: (0, 0)),
        ],
        out_specs=(
            pl.BlockSpec((rows, SUBLANES), lambda i: (i, 0)),
            pl.BlockSpec((rows, SUBLANES), lambda i: (i, 0)),
            pl.BlockSpec((SUBLANES, LANES), lambda i: (0, 0)),
            pl.BlockSpec((rows, d // 2 // LANES, LANES), lambda i: (i, 0, 0)),
        ),
        scratch_shapes=[pltpu.VMEM((SUBLANES, LANES), F32)],
        compiler_params=_params(("arbitrary",)),
        name="moe_router",
    )(x, g.reshape(1, d), wr)


MOE_SUB_ROWS = 256


def _moe_ffn_kernel(te_ref, nu_ref, tv_ref, tok_ref, tokn_ref, hp_ref, wg_ref, wu_ref, wd_ref, o_ref,
                    xg_ref, hn_ref, sem, *, tm, n_f):
    del te_ref
    i = pl.program_id(0)
    f = pl.program_id(1)
    n_used = nu_ref[0]
    slot = lax.rem(i, 2)
    per_step = tm // n_f
    extra = tm - per_step * n_f

    def row_copy(tok, r, s):
        return pltpu.make_async_copy(hp_ref.at[tok[r]], xg_ref.at[s, r], sem.at[s])

    @pl.when(jnp.logical_and(i >= n_used, f == 0))
    def _():
        o_ref[...] = jnp.zeros_like(o_ref)

    @pl.when(i < n_used)
    def _():
        @pl.when(f == 0)
        def _():
            @pl.when(i == 0)
            def _():
                def start(r, carry):
                    row_copy(tok_ref, r, 0).start()
                    return carry
                lax.fori_loop(0, tm, start, 0)

            pltpu.make_async_copy(hp_ref.at[pl.ds(0, tm)], xg_ref.at[slot], sem.at[slot]).wait()
            chunks = pltpu.einshape("tcl->ctl", xg_ref[slot])
            half = hn_ref.shape[1] // 2
            for c in range(chunks.shape[0]):
                words = chunks[c]
                lo = lax.bitcast_convert_type(jnp.left_shift(words, jnp.uint32(16)), F32)
                hi = lax.bitcast_convert_type(jnp.bitwise_and(words, jnp.uint32(0xFFFF0000)), F32)
                hn_ref[:, c * LANES:(c + 1) * LANES] = lo.astype(BF16)
                hn_ref[:, half + c * LANES:half + (c + 1) * LANES] = hi.astype(BF16)
            o_ref[...] = jnp.zeros_like(o_ref)

        @pl.when(i + 1 < n_used)
        def _():
            base = f * per_step + jnp.minimum(f, extra)
            for rr in range(per_step):
                row_copy(tokn_ref, base + rr, 1 - slot).start()
            if extra:
                @pl.when(f < extra)
                def _():
                    row_copy(tokn_ref, base + per_step, 1 - slot).start()

        n_sub = lax.shift_right_logical(tv_ref[i] + (MOE_SUB_ROWS - 1), MOE_SUB_ROWS.bit_length() - 1)
        for k in range(1, tm // MOE_SUB_ROWS + 1):
            @pl.when(n_sub == k)
            def _(k=k):
                rows = k * MOE_SUB_ROWS
                hn = hn_ref[0:rows, :]
                gate = jnp.dot(hn, wg_ref[0].astype(BF16), preferred_element_type=F32)
                up = jnp.dot(hn, wu_ref[0].astype(BF16), preferred_element_type=F32)
                h = _silu_mul(gate, up).astype(BF16)
                o_ref[0:rows, :] += jnp.dot(h, wd_ref[0].astype(BF16), preferred_element_type=F32)


def _moe_ffn(hp, tok, we_gate, we_up, we_down, tile_expert, n_used, tile_valid, *, tm, tf):
    d = 2 * hp.shape[1] * hp.shape[2]
    n_rows = tok.shape[0]
    dff = we_gate.shape[2]
    n_f = dff // tf
    n_tiles = n_rows // tm
    assert tm % MOE_SUB_ROWS == 0 and MOE_SUB_ROWS & (MOE_SUB_ROWS - 1) == 0

    def tile(i, nu):
        return jnp.minimum(i, nu[0] - 1)

    def fidx(i, f, nu):
        return jnp.where(i < nu[0], f, n_f - 1)

    grid_spec = pltpu.PrefetchScalarGridSpec(
        num_scalar_prefetch=3,
        grid=(n_tiles, n_f),
        in_specs=[
            pl.BlockSpec((tm,), lambda i, f, te, nu, tv: (tile(i, nu),), memory_space=pltpu.SMEM),
            pl.BlockSpec((tm,), lambda i, f, te, nu, tv: (tile(i + 1, nu),), memory_space=pltpu.SMEM),
            pl.BlockSpec(memory_space=pl.ANY),
            pl.BlockSpec((1, d, tf), lambda i, f, te, nu, tv: (te[tile(i, nu)], 0, fidx(i, f, nu))),
            pl.BlockSpec((1, d, tf), lambda i, f, te, nu, tv: (te[tile(i, nu)], 0, fidx(i, f, nu))),
            pl.BlockSpec((1, tf, d), lambda i, f, te, nu, tv: (te[tile(i, nu)], fidx(i, f, nu), 0)),
        ],
        out_specs=pl.BlockSpec((tm, d), lambda i, f, te, nu, tv: (i, 0)),
        scratch_shapes=[
            pltpu.VMEM((2, tm) + hp.shape[1:], jnp.uint32),
            pltpu.VMEM((tm, d), BF16),
            pltpu.SemaphoreType.DMA((2,)),
        ],
    )
    return pl.pallas_call(
        functools.partial(_moe_ffn_kernel, tm=tm, n_f=n_f),
        out_shape=jax.ShapeDtypeStruct((n_rows, d), F32),
        grid_spec=grid_spec,
        compiler_params=_params(("arbitrary", "arbitrary")),
        name="moe_ffn",
    )(tile_expert, n_used, tile_valid, tok, tok, hp, we_gate, we_up, we_down)


COMBINE_ROWS = 512


def _combine_kernel(pos_ref, x_ref, wt_ref, ys_ref, o_ref, y1_ref, y2_ref, sem):
    def row_copy(r, k, dst):
        return pltpu.make_async_copy(ys_ref.at[pl.ds(pos_ref[2 * r + k], 1)], dst.at[pl.ds(r, 1)], sem)

    def start(r, carry):
        row_copy(r, 0, y1_ref).start()
        row_copy(r, 1, y2_ref).start()
        return carry

    lax.fori_loop(0, COMBINE_ROWS, start, 0)
    for dst in (y1_ref, y2_ref):
        pltpu.make_async_copy(ys_ref.at[pl.ds(0, COMBINE_ROWS)], dst, sem).wait()
    wt = wt_ref[...]
    o_ref[...] = x_ref[...] + wt[:, 0:1] * y1_ref[...] + wt[:, 1:2] * y2_ref[...]


def _combine(x, wts, ys, pos_flat):
    t, d = x.shape
    return pl.pallas_call(
        _combine_kernel,
        out_shape=jax.ShapeDtypeStruct((t, d), F32),
        grid=(t // COMBINE_ROWS,),
        in_specs=[
            pl.BlockSpec((2 * COMBINE_ROWS,), lambda i: (i,), memory_space=pltpu.SMEM),
            pl.BlockSpec((COMBINE_ROWS, d), lambda i: (i, 0)),
            pl.BlockSpec((COMBINE_ROWS, SUBLANES), lambda i: (i, 0)),
            pl.BlockSpec(memory_space=pl.ANY),
        ],
        out_specs=pl.BlockSpec((COMBINE_ROWS, d), lambda i: (i, 0)),
        scratch_shapes=[
            pltpu.VMEM((COMBINE_ROWS, d), F32),
            pltpu.VMEM((COMBINE_ROWS, d), F32),
            pltpu.SemaphoreType.DMA(()),
        ],
        compiler_params=_params(("arbitrary",)),
        name="moe_combine",
    )(pos_flat, x, wts, ys)


def _moe_block(x, ffn_norm, router, we_gate, we_up, we_down, *, tm=1024, tf=256):
    t, d = x.shape
    idx, wts, cnt, hp = _router(x, ffn_norm, router, rows=256)
    n_tiles = (TOP_K * t) // tm + N_EXPERTS
    counts = cnt[0, :N_EXPERTS]
    padded = ((counts + tm - 1) // tm) * tm
    ends = jnp.cumsum(padded)
    offsets = ends - padded
    pos = jnp.take(offsets, idx[:, 0:2]) + idx[:, 2:4]
    pos_flat = pos.reshape(-1).astype(jnp.int32)
    tile_start = jnp.arange(n_tiles, dtype=jnp.int32) * tm
    tile_expert = jnp.minimum(jnp.sum(tile_start[:, None] >= ends[None, :], axis=1), N_EXPERTS - 1).astype(jnp.int32)
    n_used = (ends[-1:] // tm).astype(jnp.int32)
    tile_valid = jnp.clip(jnp.take(offsets + counts, tile_expert) - tile_start, 0, tm).astype(jnp.int32)
    tok = jnp.zeros((n_tiles * tm,), jnp.int32).at[pos_flat].set(jnp.repeat(jnp.arange(t, dtype=jnp.int32), TOP_K))
    ys = _moe_ffn(hp, tok, we_gate, we_up, we_down, tile_expert, n_used, tile_valid, tm=tm, tf=tf)
    return _combine(x, wts, ys, pos_flat)


def kernel(x, ev_attn_norm, ev_w_in, ev_q_norm_sb, ev_k_norm_sb, ev_q_norm_moba, ev_k_norm_moba, ev_w_o, ev_ffn_norm, ev_w_gate, ev_w_up, ev_w_down, od_rec_norm, od_w_in, od_conv_w, od_conv_b, od_w_a, od_b_a, od_w_i, od_b_i, od_lambda, od_w_o, od_ffn_norm, od_router, od_we_gate, od_we_up, od_we_down):
    batch, seq, d = x.shape
    t = batch * seq
    tm = min(1024, seq)
    h = x.reshape(t, d)
    h = _attention_layer(h, seq, batch, ev_attn_norm[0], ev_w_in[0], ev_q_norm_sb[0], ev_k_norm_sb[0],
                         ev_q_norm_moba[0], ev_k_norm_moba[0], ev_w_o[0], ev_ffn_norm[0],
                         ev_w_gate[0], ev_w_up[0], ev_w_down[0], tm=tm)
    h = _recurrent_block(h, seq, batch, od_rec_norm[0], od_w_in[0], od_conv_w[0], od_conv_b[0], od_w_a[0],
                         od_b_a[0], od_w_i[0], od_b_i[0], od_lambda[0], od_w_o[0], tm=tm)
    h = _moe_block(h, od_ffn_norm[0], od_router[0], od_we_gate[0], od_we_up[0], od_we_down[0])
    return h.reshape(batch, seq, d)
```

```python
import functools
import math
from typing import NamedTuple

import jax
import jax.numpy as jnp
from jax import lax
from jax.experimental import pallas as pl
from jax.experimental.pallas import tpu as pltpu

F32 = jnp.float32
BF16 = jnp.bfloat16

HEAD_DIM = 128
N_HEADS_SB = 8
N_HEADS_MOBA = 8
ROPE_DIM = HEAD_DIM // 4
ROPE_THETA = 500000.0
MOBA_BLOCK = 256
MOBA_TOPK = 3
LRU_BLOCKS = 8
LRU_C = 8.0
CONV_WIDTH = 4
N_EXPERTS = 8
TOP_K = 2
NORM_EPS = 1e-6
NEG_INF = -1e30

LANES = 128
SUBLANES = 8
VMEM_LIMIT_BYTES = 56 * 1024 * 1024


class _Tiles(NamedTuple):
    rows: int
    out_rows: int
    proj_cols: int
    ff_cols: int
    sb_block: int
    scan_rows: int
    route_rows: int


def _tiles(seq):
    rows = min(1024, seq)
    return _Tiles(rows=rows, out_rows=rows // 2, proj_cols=512, ff_cols=256, sb_block=256,
                  scan_rows=min(512, seq), route_rows=256)


def _params(semantics, vmem=VMEM_LIMIT_BYTES):
    return pltpu.CompilerParams(dimension_semantics=semantics, vmem_limit_bytes=vmem)


def _rms(x, g):
    ms = jnp.mean(x * x, axis=-1, keepdims=True)
    return x * lax.rsqrt(ms + NORM_EPS) * g


def _norm_matmul_kernel(x_ref, g_ref, w_ref, o_ref, hn_ref):
    @pl.when(pl.program_id(1) == 0)
    def _():
        hn_ref[...] = _rms(x_ref[...], g_ref[...]).astype(BF16)

    y = jnp.dot(hn_ref[...], w_ref[...].astype(BF16), preferred_element_type=F32)
    o_ref[...] = y.astype(o_ref.dtype)


def _qkv_kernel(x_ref, g_ref, w_ref, hg_ref, c_ref, sa_ref, sb_ref, o_ref, hn_ref, *, tn, sec_w):
    j = pl.program_id(1)

    @pl.when(j == 0)
    def _():
        hn_ref[...] = _rms(x_ref[...], g_ref[...]).astype(BF16)

    y = jnp.dot(hn_ref[...], w_ref[...].astype(BF16), preferred_element_type=F32)
    sec = j // (sec_w // tn)
    is_v = jnp.logical_or(sec == 2, sec == 5)
    is_mb = jnp.logical_or(sec == 3, sec == 4)
    is_sb = jnp.logical_or(sec == 0, sec == 1)

    def split_dot(t, mat):
        hi = t.astype(BF16)
        lo = (t - hi.astype(F32)).astype(BF16)
        return (jnp.dot(hi, mat, preferred_element_type=F32) + jnp.dot(lo, mat, preferred_element_type=F32))

    def head_norm(hh):
        t = y[:, hh * HEAD_DIM:(hh + 1) * HEAD_DIM]
        mean_mat = jnp.full((HEAD_DIM, HEAD_DIM), 1.0 / HEAD_DIM, BF16)
        ms = jnp.dot((t * t).astype(BF16), mean_mat, preferred_element_type=F32)
        return t * lax.rsqrt(ms + NORM_EPS) * hg_ref[0]

    @pl.when(is_v)
    def _():
        o_ref[...] = y.astype(o_ref.dtype)

    @pl.when(is_sb)
    def _():
        for hh in range(tn // HEAD_DIM):
            o_ref[:, hh * HEAD_DIM:(hh + 1) * HEAD_DIM] = head_norm(hh).astype(o_ref.dtype)

    @pl.when(is_mb)
    def _():
        half = ROPE_DIM // 2
        src = lax.broadcasted_iota(jnp.int32, (HEAD_DIM, 2 * HEAD_DIM), 0)
        dst = lax.broadcasted_iota(jnp.int32, (HEAD_DIM, 2 * HEAD_DIM), 1)
        from_below = jnp.logical_and(dst < HEAD_DIM, src + half == dst)
        from_above = jnp.logical_and(dst >= HEAD_DIM, src - half == dst - HEAD_DIM)
        shift_mat = jnp.where(jnp.logical_or(from_below, from_above), 1.0, 0.0).astype(BF16)
        for hh in range(tn // HEAD_DIM):
            t = head_norm(hh)
            rot = split_dot(t, shift_mat)
            t = t * c_ref[0] + rot[:, :HEAD_DIM] * sa_ref[0] + rot[:, HEAD_DIM:] * sb_ref[0]
            o_ref[:, hh * HEAD_DIM:(hh + 1) * HEAD_DIM] = t.astype(o_ref.dtype)


def _norm_matmul(x, g, w, *, tm, tn, out_dtype=BF16):
    t, d = x.shape
    n = w.shape[1]
    return pl.pallas_call(
        _norm_matmul_kernel,
        out_shape=jax.ShapeDtypeStruct((t, n), out_dtype),
        grid=(t // tm, n // tn),
        in_specs=[
            pl.BlockSpec((tm, d), lambda i, j: (i, 0)),
            pl.BlockSpec((1, d), lambda i, j: (0, 0)),
            pl.BlockSpec((d, tn), lambda i, j: (0, j)),
        ],
        out_specs=pl.BlockSpec((tm, tn), lambda i, j: (i, j)),
        scratch_shapes=[pltpu.VMEM((tm, d), BF16)],
        compiler_params=_params(("parallel", "arbitrary")),
        name="norm_matmul",
    )(x, g.reshape(1, d), w)


def _rope_tables(seq):
    half = ROPE_DIM // 2
    inv_freq = ROPE_THETA ** (-jnp.arange(0, ROPE_DIM, 2, dtype=F32) / ROPE_DIM)
    ang = jnp.arange(seq, dtype=F32)[:, None] * inv_freq[None, :]
    cos, sin = jnp.cos(ang), jnp.sin(ang)
    ones = jnp.ones((seq, HEAD_DIM - ROPE_DIM), F32)
    zeros = jnp.zeros((seq, HEAD_DIM - ROPE_DIM), F32)
    zh = jnp.zeros((seq, half), F32)
    c = jnp.concatenate([cos, cos, ones], axis=1)
    sa = jnp.concatenate([zh, sin, zeros], axis=1)
    sb = jnp.concatenate([-sin, zh, zeros], axis=1)
    ident = jnp.ones((seq, HEAD_DIM), F32)
    z = jnp.zeros((seq, HEAD_DIM), F32)
    return jnp.stack([ident, c]), jnp.stack([z, sa]), jnp.stack([z, sb])


def _qkv_proj(x, g, w, head_gains, seq, *, tm, tn):
    t, d = x.shape
    n = w.shape[1]
    sec_w = n // 6
    c, sa, sb = _rope_tables(seq)
    n_s = seq // tm
    kern = functools.partial(_qkv_kernel, tn=tn, sec_w=sec_w)
    per_sec = sec_w // tn
    return pl.pallas_call(
        kern,
        out_shape=jax.ShapeDtypeStruct((t, n), BF16),
        grid=(t // tm, n // tn),
        in_specs=[
            pl.BlockSpec((tm, d), lambda i, j: (i, 0)),
            pl.BlockSpec((1, d), lambda i, j: (0, 0)),
            pl.BlockSpec((d, tn), lambda i, j: (0, j)),
            pl.BlockSpec((1, 1, HEAD_DIM), lambda i, j: (j // per_sec, 0, 0)),
            pl.BlockSpec((1, tm, HEAD_DIM), lambda i, j: ((j // per_sec) // 3, i % n_s, 0)),
            pl.BlockSpec((1, tm, HEAD_DIM), lambda i, j: ((j // per_sec) // 3, i % n_s, 0)),
            pl.BlockSpec((1, tm, HEAD_DIM), lambda i, j: ((j // per_sec) // 3, i % n_s, 0)),
        ],
        out_specs=pl.BlockSpec((tm, tn), lambda i, j: (i, j)),
        scratch_shapes=[pltpu.VMEM((tm, d), BF16)],
        compiler_params=_params(("parallel", "arbitrary")),
        name="qkv_proj",
    )(x, g.reshape(1, d), w, head_gains, c, sa, sb)


def _matmul_res_kernel(*refs, n_a):
    a_refs = refs[:n_a]
    w_ref, r_ref, o_ref, wb_ref = refs[n_a:]

    @pl.when(pl.program_id(0) == 0)
    def _():
        wb_ref[...] = w_ref[...].astype(BF16)

    acc = r_ref[...]
    k0 = 0
    for a_ref in a_refs:
        k = a_ref.shape[1]
        acc = acc + jnp.dot(a_ref[...], wb_ref[k0:k0 + k, :], preferred_element_type=F32)
        k0 += k
    o_ref[...] = acc


def _matmul_res(a_list, w, res, *, tm):
    t = res.shape[0]
    k, n = w.shape
    kern = functools.partial(_matmul_res_kernel, n_a=len(a_list))
    a_specs = [pl.BlockSpec((tm, a.shape[1]), lambda i: (i, 0)) for a in a_list]
    return pl.pallas_call(
        kern,
        out_shape=jax.ShapeDtypeStruct((t, n), F32),
        grid=(t // tm,),
        in_specs=a_specs + [
            pl.BlockSpec((k, n), lambda i: (0, 0), pipeline_mode=pl.Buffered(1)),
            pl.BlockSpec((tm, n), lambda i: (i, 0)),
        ],
        out_specs=pl.BlockSpec((tm, n), lambda i: (i, 0)),
        scratch_shapes=[pltpu.VMEM((k, n), BF16)],
        compiler_params=_params(("arbitrary",)),
        name="matmul_res",
    )(*a_list, w, res)


def _silu_mul(gate, up):
    return gate * (1.0 / (1.0 + jnp.exp(-gate))) * up


def _ffn_kernel(x_ref, g_ref, wg_ref, wu_ref, wd_ref, o_ref, hn_ref):
    @pl.when(pl.program_id(1) == 0)
    def _():
        x = x_ref[...]
        hn_ref[...] = _rms(x, g_ref[...]).astype(BF16)
        o_ref[...] = x

    hn = hn_ref[...]
    gate = jnp.dot(hn, wg_ref[...].astype(BF16), preferred_element_type=F32)
    up = jnp.dot(hn, wu_ref[...].astype(BF16), preferred_element_type=F32)
    h = _silu_mul(gate, up).astype(BF16)
    o_ref[...] += jnp.dot(h, wd_ref[...].astype(BF16), preferred_element_type=F32)


def _ffn(x, g, wg, wu, wd, *, tm, tf):
    t, d = x.shape
    dff = wg.shape[1]
    return pl.pallas_call(
        _ffn_kernel,
        out_shape=jax.ShapeDtypeStruct((t, d), F32),
        grid=(t // tm, dff // tf),
        in_specs=[
            pl.BlockSpec((tm, d), lambda i, f: (i, 0), pipeline_mode=pl.Buffered(1)),
            pl.BlockSpec((1, d), lambda i, f: (0, 0)),
            pl.BlockSpec((d, tf), lambda i, f: (0, f)),
            pl.BlockSpec((d, tf), lambda i, f: (0, f)),
            pl.BlockSpec((tf, d), lambda i, f: (f, 0)),
        ],
        out_specs=pl.BlockSpec((tm, d), lambda i, f: (i, 0)),
        scratch_shapes=[pltpu.VMEM((tm, d), BF16)],
        compiler_params=_params(("parallel", "arbitrary")),
        name="ffn_swiglu",
    )(x, g.reshape(1, d), wg, wu, wd)


def _transpose_v(v_ref, vt_ref, blk):
    for c in range(v_ref.shape[0] // blk):
        vt_ref[:, c * blk:(c + 1) * blk] = v_ref[c * blk:(c + 1) * blk, :].astype(F32).T.astype(vt_ref.dtype)


SB_DEAD_LOG_WEIGHT = -110.0


def _sb_kernel(q_ref, k_ref, v_ref, o_ref, vt_ref, acc_ref, *, blk):
    qi = pl.program_id(2)

    @pl.when(qi == 0)
    def _():
        _transpose_v(v_ref, vt_ref, blk)

    q = q_ref[...]
    key = lax.broadcasted_iota(jnp.int32, (blk, blk), 0)
    qry = lax.broadcasted_iota(jnp.int32, (blk, blk), 1)
    later_mat = jnp.where(qry > key, 1.0, 0.0).astype(BF16)

    def log_weights(kj, diagonal):
        start = pl.multiple_of(kj * blk, blk)
        z = lax.dot_general(k_ref[pl.ds(start, blk), :], q, (((1,), (1,)), ((), ())),
                            preferred_element_type=F32)
        sp = jnp.log(1.0 + jnp.exp(-jnp.abs(z)))
        log_beta = jnp.minimum(z, 0.0) - sp
        log_keep = log_beta - z
        if diagonal:
            past = key < qry
            log_keep = jnp.where(past, log_keep, 0.0)
        lk_hi = log_keep.astype(BF16)
        lk_lo = (log_keep - lk_hi.astype(F32)).astype(BF16)
        later = (jnp.dot(later_mat, lk_hi, preferred_element_type=F32)
                 + jnp.dot(later_mat, lk_lo, preferred_element_type=F32))
        lw = log_beta + later
        if diagonal:
            lw = jnp.where(past, lw, NEG_INF)
        return start, lw, jnp.sum(log_keep, axis=0, keepdims=True)

    def weighted_values(start, lw, c):
        w = jnp.exp(lw + c).astype(BF16)
        return jnp.dot(vt_ref[:, pl.ds(start, blk)], w, preferred_element_type=F32)

    has_prev = qi > 0
    st0, lw0, sum0 = log_weights(qi, True)
    st1, lw1, sum1 = log_weights(jnp.maximum(qi - 1, 0), False)
    acc_ref[...] = (weighted_values(st0, lw0, jnp.zeros((1, blk), F32))
                    + weighted_values(st1, lw1, jnp.where(has_prev, sum0, NEG_INF)))
    c = sum0 + jnp.where(has_prev, sum1, 0.0)

    def cond(carry):
        kj, _, alive = carry
        return jnp.logical_and(kj >= 0, alive)

    def body(carry):
        kj, c, _ = carry
        st, lw, sm = log_weights(kj, False)
        acc_ref[...] += weighted_values(st, lw, c)
        c = c + sm
        return kj - 1, c, jnp.max(c) > SB_DEAD_LOG_WEIGHT

    lax.while_loop(cond, body, (qi - 2, c, jnp.max(c) > SB_DEAD_LOG_WEIGHT))
    o_ref[...] = acc_ref[...].T.astype(o_ref.dtype)


def _sb_attention(qkv, batch, seq, *, blk):
    t = qkv.shape[0]
    bq = blk
    nq = seq // bq
    kern = functools.partial(_sb_kernel, blk=blk)
    return pl.pallas_call(
        kern,
        out_shape=jax.ShapeDtypeStruct((t, N_HEADS_SB * HEAD_DIM), BF16),
        grid=(batch, N_HEADS_SB, nq),
        in_specs=[
            pl.BlockSpec((bq, HEAD_DIM), lambda b, h, i: (b * nq + i, h)),
            pl.BlockSpec((seq, HEAD_DIM), lambda b, h, i: (b, N_HEADS_SB + h)),
            pl.BlockSpec((seq, HEAD_DIM), lambda b, h, i: (b, 2 * N_HEADS_SB + h)),
        ],
        out_specs=pl.BlockSpec((bq, HEAD_DIM), lambda b, h, i: (b * nq + i, h)),
        scratch_shapes=[pltpu.VMEM((HEAD_DIM, seq), BF16), pltpu.VMEM((HEAD_DIM, bq), F32)],
        compiler_params=_params(("parallel", "parallel", "arbitrary")),
        name="sb_attention",
    )(qkv, qkv, qkv)


MOBA_GROUP = 2


def _moba_kernel(q_ref, k_ref, v_ref, o_ref, vt_ref, kmean_ref, *, blk, seq):
    qi = pl.program_id(2)
    n_blocks = seq // blk

    @pl.when(qi == 0)
    def _():
        _transpose_v(v_ref, vt_ref, blk)
        r = lax.broadcasted_iota(jnp.int32, (n_blocks, seq), 0)
        s = lax.broadcasted_iota(jnp.int32, (n_blocks, seq), 1)
        avg = jnp.where(s // blk == r, 1.0 / blk, 0.0).astype(BF16)
        kmean_ref[...] = jnp.dot(avg, k_ref[...], preferred_element_type=F32)

    q = q_ref[...]
    gate = lax.dot_general(kmean_ref[...], q.astype(F32), (((1,), (1,)), ((), ())),
                           precision=lax.Precision.HIGHEST, preferred_element_type=F32)
    blk_id = lax.broadcasted_iota(jnp.int32, (n_blocks, blk), 0)
    lowest = jnp.float32(-3.0e38)
    is_past = blk_id < qi
    g = jnp.where(is_past, gate, lowest)
    bias = jnp.full((n_blocks, blk), NEG_INF, F32)
    for _ in range(MOBA_TOPK):
        mx = jnp.max(g, axis=0, keepdims=True)
        first = jnp.min(jnp.where(g == mx, blk_id, n_blocks), axis=0, keepdims=True)
        pick = blk_id == first
        bias = jnp.where(jnp.logical_and(pick, is_past), 0.0, bias)
        g = jnp.where(pick, lowest, g)
    bias = jnp.where(blk_id == qi, 0.0, bias)

    key = lax.broadcasted_iota(jnp.int32, (blk, blk), 0)
    qry = lax.broadcasted_iota(jnp.int32, (blk, blk), 1) + qi * blk

    def attend(extent):
        s = lax.dot_general(k_ref[0:extent * blk, :], q, (((1,), (1,)), ((), ())),
                            preferred_element_type=F32)
        parts = []
        for n in range(extent):
            part = s[n * blk:(n + 1) * blk, :] + bias[n:n + 1, :]
            if n >= extent - MOBA_GROUP:
                part = jnp.where(key + n * blk <= qry, part, NEG_INF)
            parts.append(part)
        s = jnp.concatenate(parts, axis=0)
        m = jnp.max(s, axis=0, keepdims=True)
        p = jnp.exp2(s - m)
        l = jnp.sum(p, axis=0, keepdims=True)
        acc = jnp.dot(vt_ref[:, 0:extent * blk], p.astype(BF16), preferred_element_type=F32)
        o_ref[...] = (acc / l).T.astype(o_ref.dtype)

    n_groups = -(-n_blocks // MOBA_GROUP)
    for grp in range(n_groups):
        @pl.when(qi // MOBA_GROUP == grp)
        def _(grp=grp):
            attend(min((grp + 1) * MOBA_GROUP, n_blocks))


def _moba_attention(qkv, batch, seq):
    t = qkv.shape[0]
    blk = MOBA_BLOCK
    nq = seq // blk
    col0 = 3 * N_HEADS_SB
    kern = functools.partial(_moba_kernel, blk=blk, seq=seq)
    return pl.pallas_call(
        kern,
        out_shape=jax.ShapeDtypeStruct((t, N_HEADS_MOBA * HEAD_DIM), BF16),
        grid=(batch, N_HEADS_MOBA, nq),
        in_specs=[
            pl.BlockSpec((blk, HEAD_DIM), lambda b, h, i: (b * nq + i, col0 + h)),
            pl.BlockSpec((seq, HEAD_DIM), lambda b, h, i: (b, col0 + N_HEADS_MOBA + h)),
            pl.BlockSpec((seq, HEAD_DIM), lambda b, h, i: (b, col0 + 2 * N_HEADS_MOBA + h)),
        ],
        out_specs=pl.BlockSpec((blk, HEAD_DIM), lambda b, h, i: (b * nq + i, h)),
        scratch_shapes=[
            pltpu.VMEM((HEAD_DIM, seq), BF16),
            pltpu.VMEM((seq // blk, HEAD_DIM), F32),
        ],
        compiler_params=_params(("parallel", "parallel", "arbitrary")),
        name="moba_attention",
    )(qkv, qkv, qkv)


def _attention_layer(x, seq, batch, attn_norm, w_in, qn_sb, kn_sb, qn_mb, kn_mb, w_o,
                     ffn_norm, w_gate, w_up, w_down, tiles):
    scale = HEAD_DIM ** -0.5
    ones = jnp.ones((HEAD_DIM,), F32)
    head_gains = jnp.stack([qn_sb * scale, kn_sb, ones, qn_mb * (scale * math.log2(math.e)), kn_mb, ones])
    head_gains = head_gains.reshape(6, 1, HEAD_DIM)
    qkv = _qkv_proj(x, attn_norm, w_in, head_gains, seq, tm=tiles.rows, tn=tiles.proj_cols)
    o_sb = _sb_attention(qkv, batch, seq, blk=tiles.sb_block)
    o_mb = _moba_attention(qkv, batch, seq)
    x = _matmul_res([o_sb, o_mb], w_o, x, tm=tiles.out_rows)
    return _ffn(x, ffn_norm, w_gate, w_up, w_down, tm=tiles.rows, tf=tiles.ff_cols)


def _sigmoid(x):
    return 1.0 / (1.0 + jnp.exp(-x))


def _gelu_tanh(x):
    return 0.5 * x * (1.0 + jnp.tanh(math.sqrt(2.0 / math.pi) * (x + 0.044715 * (x * x * x))))


def _rglru_kernel(xb_ref, gt_ref, cw_ref, cb_ref, wa_ref, ba_ref, wi_ref, bi_ref, lam_ref, o_ref,
                  xpad_ref, h_ref, a_ref, b_ref, *, tt, pad):
    ti = pl.program_id(2)

    @pl.when(ti == 0)
    def _():
        xpad_ref[0:SUBLANES, :] = jnp.zeros((SUBLANES, xpad_ref.shape[1]), F32)
        h_ref[...] = jnp.zeros_like(h_ref)
        a_ref[0:pad, :] = jnp.ones((pad, a_ref.shape[1]), F32)
        b_ref[0:pad, :] = jnp.zeros((pad, b_ref.shape[1]), F32)

    xb = xb_ref[...].astype(F32)
    xpad_ref[SUBLANES:SUBLANES + tt, :] = xb
    cw = cw_ref[...]
    xc = cb_ref[...] + xb * cw[CONV_WIDTH - 1:CONV_WIDTH, :]
    for tap in range(CONV_WIDTH - 1):
        off = SUBLANES - (CONV_WIDTH - 1) + tap
        xc = xc + xpad_ref[off:off + tt, :] * cw[tap:tap + 1, :]
    xpad_ref[0:SUBLANES, :] = xpad_ref[tt:tt + SUBLANES, :]

    xcb = xc.astype(BF16)
    r = _sigmoid(jnp.dot(xcb, wa_ref[0].astype(BF16), preferred_element_type=F32) + ba_ref[...])
    ig = _sigmoid(jnp.dot(xcb, wi_ref[0].astype(BF16), preferred_element_type=F32) + bi_ref[...])
    neg_lam = -lam_ref[...]
    sp = jnp.maximum(neg_lam, 0.0) + jnp.log1p(jnp.exp(-jnp.abs(neg_lam)))
    log_a = (-LRU_C * r) * sp
    a = jnp.exp(log_a)
    u = jnp.sqrt(-jnp.tanh(log_a) * (a * a + 1.0)) * (ig * xc)

    a_ref[pad:pad + tt, :] = a
    b_ref[pad:pad + tt, :] = u
    s = 1
    while s < tt:
        a_cur = a_ref[pad:pad + tt, :]
        b_cur = b_ref[pad:pad + tt, :]
        a_sh = a_ref[pad - s:pad - s + tt, :]
        b_sh = b_ref[pad - s:pad - s + tt, :]
        b_ref[pad:pad + tt, :] = a_cur * b_sh + b_cur
        a_ref[pad:pad + tt, :] = a_cur * a_sh
        s *= 2
    h = a_ref[pad:pad + tt, :] * h_ref[0:1, :] + b_ref[pad:pad + tt, :]
    h_ref[0:1, :] = h[tt - 1:tt, :]
    o_ref[...] = (h * _gelu_tanh(gt_ref[...].astype(F32))).astype(o_ref.dtype)


def _rglru(xbg, batch, seq, conv_w, conv_b, w_a, b_a, w_i, b_i, lam, *, tt):
    t = xbg.shape[0]
    width = xbg.shape[1] // 2
    bw = width // LRU_BLOCKS
    n_t = seq // tt
    pad = tt // 2
    kern = functools.partial(_rglru_kernel, tt=tt, pad=pad)
    vec = lambda a: a.reshape(1, width)
    vspec = pl.BlockSpec((1, bw), lambda b, c, i: (0, c))
    return pl.pallas_call(
        kern,
        out_shape=jax.ShapeDtypeStruct((t, width), BF16),
        grid=(batch, LRU_BLOCKS, n_t),
        in_specs=[
            pl.BlockSpec((tt, bw), lambda b, c, i: (b * n_t + i, c)),
            pl.BlockSpec((tt, bw), lambda b, c, i: (b * n_t + i, LRU_BLOCKS + c)),
            pl.BlockSpec((CONV_WIDTH, bw), lambda b, c, i: (0, c)),
            vspec,
            pl.BlockSpec((1, bw, bw), lambda b, c, i: (c, 0, 0)),
            vspec,
            pl.BlockSpec((1, bw, bw), lambda b, c, i: (c, 0, 0)),
            vspec,
            vspec,
        ],
        out_specs=pl.BlockSpec((tt, bw), lambda b, c, i: (b * n_t + i, c)),
        scratch_shapes=[
            pltpu.VMEM((tt + 2 * SUBLANES, bw), F32),
            pltpu.VMEM((SUBLANES, bw), F32),
            pltpu.VMEM((pad + tt, bw), F32),
            pltpu.VMEM((pad + tt, bw), F32),
        ],
        compiler_params=_params(("parallel", "parallel", "arbitrary")),
        name="rglru",
    )(xbg, xbg, conv_w, vec(conv_b), w_a, vec(b_a), w_i, vec(b_i), vec(lam))


def _recurrent_block(x, seq, batch, rec_norm, w_in, conv_w, conv_b, w_a, b_a, w_i, b_i, lam, w_o, tiles):
    xbg = _norm_matmul(x, rec_norm, w_in, tm=tiles.rows, tn=tiles.proj_cols)
    y = _rglru(xbg, batch, seq, conv_w, conv_b, w_a, b_a, w_i, b_i, lam, tt=tiles.scan_rows)
    return _matmul_res([y], w_o, x, tm=tiles.out_rows)


def _router_kernel(x_ref, g_ref, wr_ref, idx_ref, wt_ref, cnt_ref, hp_ref, run_ref, *, rows):
    @pl.when(pl.program_id(0) == 0)
    def _():
        run_ref[...] = jnp.zeros_like(run_ref)

    hn = _rms(x_ref[...], g_ref[...])
    bits = lax.bitcast_convert_type(hn.astype(BF16).astype(F32), jnp.uint32)
    half = bits.shape[1] // 2
    words = jnp.bitwise_or(jnp.bitwise_and(bits[:, half:], jnp.uint32(0xFFFF0000)),
                           jnp.right_shift(bits[:, :half], jnp.uint32(16)))
    for c in range(hp_ref.shape[1]):
        hp_ref[:, c, :] = words[:, c * LANES:(c + 1) * LANES]
    logits = jnp.dot(hn, wr_ref[...], precision=lax.Precision.HIGHEST, preferred_element_type=F32)
    lane = lax.broadcasted_iota(jnp.int32, (rows, LANES), 1)
    lowest = jnp.float32(-3.0e38)
    logits = jnp.where(lane < N_EXPERTS, logits, lowest)
    m1 = jnp.max(logits, axis=1, keepdims=True)
    e1 = jnp.min(jnp.where(logits == m1, lane, LANES), axis=1, keepdims=True)
    rest = jnp.where(lane == e1, lowest, logits)
    m2 = jnp.max(rest, axis=1, keepdims=True)
    e2 = jnp.min(jnp.where(rest == m2, lane, LANES), axis=1, keepdims=True)
    ex = jnp.exp(m2 - m1)
    w1 = 1.0 / (1.0 + ex)
    w2 = ex / (1.0 + ex)

    onehot = jnp.where(jnp.logical_or(lane == e1, lane == e2), 1.0, 0.0)
    rr = lax.broadcasted_iota(jnp.int32, (rows, rows), 0)
    cc = lax.broadcasted_iota(jnp.int32, (rows, rows), 1)
    before = jnp.where(cc < rr, 1.0, 0.0).astype(BF16)
    rank = jnp.dot(before, onehot.astype(BF16), preferred_element_type=F32) + run_ref[0:1, :]
    r1 = jnp.sum(jnp.where(lane == e1, rank, 0.0), axis=1, keepdims=True).astype(jnp.int32)
    r2 = jnp.sum(jnp.where(lane == e2, rank, 0.0), axis=1, keepdims=True).astype(jnp.int32)
    run_ref[0:1, :] = run_ref[0:1, :] + jnp.sum(onehot, axis=0, keepdims=True)

    packed = jnp.where(lane == 0, e1, jnp.where(lane == 1, e2, jnp.where(lane == 2, r1, r2)))
    idx_ref[...] = packed[:, 0:idx_ref.shape[1]]
    wts = jnp.where(lane == 0, w1, w2)
    wt_ref[...] = wts[:, 0:wt_ref.shape[1]]
    cnt_ref[...] = run_ref[...].astype(jnp.int32)


def _router(x, g, router, *, rows):
    t, d = x.shape
    wr = jnp.pad(router, ((0, 0), (0, LANES - router.shape[1])))
    kern = functools.partial(_router_kernel, rows=rows)
    return pl.pallas_call(
        kern,
        out_shape=(
            jax.ShapeDtypeStruct((t, SUBLANES), jnp.int32),
            jax.ShapeDtypeStruct((t, SUBLANES), F32),
            jax.ShapeDtypeStruct((SUBLANES, LANES), jnp.int32),
            jax.ShapeDtypeStruct((t, d // 2 // LANES, LANES), jnp.uint32),
        ),
        grid=(t // rows,),
        in_specs=[
            pl.BlockSpec((rows, d), lambda i: (i, 0)),
            pl.BlockSpec((1, d), lambda i: (0, 0)),
            pl.BlockSpec((d, LANES), lambda i: (0, 0)),
        ],
        out_specs=(
            pl.BlockSpec((rows, SUBLANES), lambda i: (i, 0)),
            pl.BlockSpec((rows, SUBLANES), lambda i: (i, 0)),
            pl.BlockSpec((SUBLANES, LANES), lambda i: (0, 0)),
            pl.BlockSpec((rows, d // 2 // LANES, LANES), lambda i: (i, 0, 0)),
        ),
        scratch_shapes=[pltpu.VMEM((SUBLANES, LANES), F32)],
        compiler_params=_params(("arbitrary",)),
        name="moe_router",
    )(x, g.reshape(1, d), wr)


MOE_SUB_ROWS = 256


def _moe_ffn_kernel(te_ref, nu_ref, tv_ref, tok_ref, tokn_ref, hp_ref, wg_ref, wu_ref, wd_ref, o_ref,
                    xg_ref, hn_ref, sem, *, tm, n_f):
    del te_ref
    i = pl.program_id(0)
    f = pl.program_id(1)
    n_used = nu_ref[0]
    slot = lax.rem(i, 2)
    per_step = tm // n_f
    extra = tm - per_step * n_f

    def row_copy(tok, r, s):
        return pltpu.make_async_copy(hp_ref.at[tok[r]], xg_ref.at[s, r], sem.at[s])

    @pl.when(jnp.logical_and(i >= n_used, f == 0))
    def _():
        o_ref[...] = jnp.zeros_like(o_ref)

    @pl.when(i < n_used)
    def _():
        @pl.when(f == 0)
        def _():
            @pl.when(i == 0)
            def _():
                def start(r, carry):
                    row_copy(tok_ref, r, 0).start()
                    return carry
                lax.fori_loop(0, tm, start, 0)

            pltpu.make_async_copy(hp_ref.at[pl.ds(0, tm)], xg_ref.at[slot], sem.at[slot]).wait()
            chunks = pltpu.einshape("tcl->ctl", xg_ref[slot])
            half = hn_ref.shape[1] // 2
            for c in range(chunks.shape[0]):
                words = chunks[c]
                lo = lax.bitcast_convert_type(jnp.left_shift(words, jnp.uint32(16)), F32)
                hi = lax.bitcast_convert_type(jnp.bitwise_and(words, jnp.uint32(0xFFFF0000)), F32)
                hn_ref[:, c * LANES:(c + 1) * LANES] = lo.astype(BF16)
                hn_ref[:, half + c * LANES:half + (c + 1) * LANES] = hi.astype(BF16)
            o_ref[...] = jnp.zeros_like(o_ref)

        @pl.when(i + 1 < n_used)
        def _():
            base = f * per_step + jnp.minimum(f, extra)
            for rr in range(per_step):
                row_copy(tokn_ref, base + rr, 1 - slot).start()
            if extra:
                @pl.when(f < extra)
                def _():
                    row_copy(tokn_ref, base + per_step, 1 - slot).start()

        n_sub = lax.shift_right_logical(tv_ref[i] + (MOE_SUB_ROWS - 1), MOE_SUB_ROWS.bit_length() - 1)
        for k in range(1, tm // MOE_SUB_ROWS + 1):
            @pl.when(n_sub == k)
            def _(k=k):
                rows = k * MOE_SUB_ROWS
                hn = hn_ref[0:rows, :]
                gate = jnp.dot(hn, wg_ref[0].astype(BF16), preferred_element_type=F32)
                up = jnp.dot(hn, wu_ref[0].astype(BF16), preferred_element_type=F32)
                h = _silu_mul(gate, up).astype(BF16)
                o_ref[0:rows, :] += jnp.dot(h, wd_ref[0].astype(BF16), preferred_element_type=F32)


def _moe_ffn(hp, tok, we_gate, we_up, we_down, tile_expert, n_used, tile_valid, *, tm, tf):
    d = 2 * hp.shape[1] * hp.shape[2]
    n_rows = tok.shape[0]
    dff = we_gate.shape[2]
    n_f = dff // tf
    n_tiles = n_rows // tm
    assert tm % MOE_SUB_ROWS == 0 and MOE_SUB_ROWS & (MOE_SUB_ROWS - 1) == 0

    def tile(i, nu):
        return jnp.maximum(jnp.minimum(i, nu[0] - 1), 0)

    def fidx(i, f, nu):
        return jnp.where(i < nu[0], f, n_f - 1)

    grid_spec = pltpu.PrefetchScalarGridSpec(
        num_scalar_prefetch=3,
        grid=(n_tiles, n_f),
        in_specs=[
            pl.BlockSpec((tm,), lambda i, f, te, nu, tv: (tile(i, nu),), memory_space=pltpu.SMEM),
            pl.BlockSpec((tm,), lambda i, f, te, nu, tv: (tile(i + 1, nu),), memory_space=pltpu.SMEM),
            pl.BlockSpec(memory_space=pl.ANY),
            pl.BlockSpec((1, d, tf), lambda i, f, te, nu, tv: (te[tile(i, nu)], 0, fidx(i, f, nu))),
            pl.BlockSpec((1, d, tf), lambda i, f, te, nu, tv: (te[tile(i, nu)], 0, fidx(i, f, nu))),
            pl.BlockSpec((1, tf, d), lambda i, f, te, nu, tv: (te[tile(i, nu)], fidx(i, f, nu), 0)),
        ],
        out_specs=pl.BlockSpec((tm, d), lambda i, f, te, nu, tv: (i, 0)),
        scratch_shapes=[
            pltpu.VMEM((2, tm) + hp.shape[1:], jnp.uint32),
            pltpu.VMEM((tm, d), BF16),
            pltpu.SemaphoreType.DMA((2,)),
        ],
    )
    return pl.pallas_call(
        functools.partial(_moe_ffn_kernel, tm=tm, n_f=n_f),
        out_shape=jax.ShapeDtypeStruct((n_rows, d), F32),
        grid_spec=grid_spec,
        compiler_params=_params(("arbitrary", "arbitrary")),
        name="moe_ffn",
    )(tile_expert, n_used, tile_valid, tok, tok, hp, we_gate, we_up, we_down)


COMBINE_ROWS = 512


def _combine_kernel(pos_ref, x_ref, wt_ref, ys_ref, o_ref, y1_ref, y2_ref, sem):
    def row_copy(r, k, dst):
        return pltpu.make_async_copy(ys_ref.at[pl.ds(pos_ref[2 * r + k], 1)], dst.at[pl.ds(r, 1)], sem)

    def start(r, carry):
        row_copy(r, 0, y1_ref).start()
        row_copy(r, 1, y2_ref).start()
        return carry

    lax.fori_loop(0, COMBINE_ROWS, start, 0)
    for dst in (y1_ref, y2_ref):
        pltpu.make_async_copy(ys_ref.at[pl.ds(0, COMBINE_ROWS)], dst, sem).wait()
    wt = wt_ref[...]
    o_ref[...] = x_ref[...] + wt[:, 0:1] * y1_ref[...] + wt[:, 1:2] * y2_ref[...]


def _combine(x, wts, ys, pos_flat):
    t, d = x.shape
    return pl.pallas_call(
        _combine_kernel,
        out_shape=jax.ShapeDtypeStruct((t, d), F32),
        grid=(t // COMBINE_ROWS,),
        in_specs=[
            pl.BlockSpec((2 * COMBINE_ROWS,), lambda i: (i,), memory_space=pltpu.SMEM),
            pl.BlockSpec((COMBINE_ROWS, d), lambda i: (i, 0)),
            pl.BlockSpec((COMBINE_ROWS, SUBLANES), lambda i: (i, 0)),
            pl.BlockSpec(memory_space=pl.ANY),
        ],
        out_specs=pl.BlockSpec((COMBINE_ROWS, d), lambda i: (i, 0)),
        scratch_shapes=[
            pltpu.VMEM((COMBINE_ROWS, d), F32),
            pltpu.VMEM((COMBINE_ROWS, d), F32),
            pltpu.SemaphoreType.DMA(()),
        ],
        compiler_params=_params(("arbitrary",)),
        name="moe_combine",
    )(pos_flat, x, wts, ys)


def _moe_block(x, ffn_norm, router, we_gate, we_up, we_down, tiles):
    t, d = x.shape
    tm, tf = tiles.rows, tiles.ff_cols
    idx, wts, cnt, hp = _router(x, ffn_norm, router, rows=tiles.route_rows)
    n_tiles = (TOP_K * t) // tm + N_EXPERTS
    counts = cnt[0, :N_EXPERTS]
    padded = ((counts + tm - 1) // tm) * tm
    ends = jnp.cumsum(padded)
    offsets = ends - padded
    pos = jnp.take(offsets, idx[:, 0:2]) + idx[:, 2:4]
    pos_flat = pos.reshape(-1).astype(jnp.int32)
    tile_start = jnp.arange(n_tiles, dtype=jnp.int32) * tm
    tile_expert = jnp.minimum(jnp.sum(tile_start[:, None] >= ends[None, :], axis=1), N_EXPERTS - 1).astype(jnp.int32)
    n_used = (ends[-1:] // tm).astype(jnp.int32)
    tile_valid = jnp.clip(jnp.take(offsets + counts, tile_expert) - tile_start, 0, tm).astype(jnp.int32)
    tok = jnp.zeros((n_tiles * tm,), jnp.int32).at[pos_flat].set(jnp.repeat(jnp.arange(t, dtype=jnp.int32), TOP_K))
    ys = _moe_ffn(hp, tok, we_gate, we_up, we_down, tile_expert, n_used, tile_valid, tm=tm, tf=tf)
    return _combine(x, wts, ys, pos_flat)


def kernel(x, ev_attn_norm, ev_w_in, ev_q_norm_sb, ev_k_norm_sb, ev_q_norm_moba, ev_k_norm_moba, ev_w_o, ev_ffn_norm, ev_w_gate, ev_w_up, ev_w_down, od_rec_norm, od_w_in, od_conv_w, od_conv_b, od_w_a, od_b_a, od_w_i, od_b_i, od_lambda, od_w_o, od_ffn_norm, od_router, od_we_gate, od_we_up, od_we_down):
    batch, seq, d = x.shape
    t = batch * seq
    tiles = _tiles(seq)
    h = x.reshape(t, d)
    h = _attention_layer(h, seq, batch, ev_attn_norm[0], ev_w_in[0], ev_q_norm_sb[0], ev_k_norm_sb[0],
                         ev_q_norm_moba[0], ev_k_norm_moba[0], ev_w_o[0], ev_ffn_norm[0],
                         ev_w_gate[0], ev_w_up[0], ev_w_down[0], tiles)
    h = _recurrent_block(h, seq, batch, od_rec_norm[0], od_w_in[0], od_conv_w[0], od_conv_b[0], od_w_a[0],
                         od_b_a[0], od_w_i[0], od_b_i[0], od_lambda[0], od_w_o[0], tiles)
    h = _moe_block(h, od_ffn_norm[0], od_router[0], od_we_gate[0], od_we_up[0], od_we_down[0], tiles)
    return h.reshape(batch, seq, d)
```

```python
import functools
import math
from typing import NamedTuple

import jax
import jax.numpy as jnp
from jax import lax
from jax.experimental import pallas as pl
from jax.experimental.pallas import tpu as pltpu

F32 = jnp.float32
BF16 = jnp.bfloat16

HEAD_DIM = 128
N_HEADS_SB = 8
N_HEADS_MOBA = 8
ROPE_DIM = HEAD_DIM // 4
ROPE_THETA = 500000.0
MOBA_BLOCK = 256
MOBA_TOPK = 3
LRU_BLOCKS = 8
LRU_C = 8.0
CONV_WIDTH = 4
N_EXPERTS = 8
TOP_K = 2
NORM_EPS = 1e-6
NEG_INF = -1e30

LANES = 128
SUBLANES = 8
VMEM_LIMIT_BYTES = 56 * 1024 * 1024


class _Tiles(NamedTuple):
    rows: int
    out_rows: int
    proj_cols: int
    ff_cols: int
    sb_block: int
    scan_rows: int
    route_rows: int


def _tiles(seq):
    rows = min(1024, seq)
    return _Tiles(rows=rows, out_rows=rows // 2, proj_cols=512, ff_cols=256, sb_block=256,
                  scan_rows=min(512, seq), route_rows=256)


def _params(semantics, vmem=VMEM_LIMIT_BYTES):
    return pltpu.CompilerParams(dimension_semantics=semantics, vmem_limit_bytes=vmem)


def _rms(x, g):
    ms = jnp.mean(x * x, axis=-1, keepdims=True)
    return x * lax.rsqrt(ms + NORM_EPS) * g


def _norm_matmul_kernel(x_ref, g_ref, w_ref, o_ref, hn_ref):
    @pl.when(pl.program_id(1) == 0)
    def _():
        hn_ref[...] = _rms(x_ref[...], g_ref[...]).astype(BF16)

    y = jnp.dot(hn_ref[...], w_ref[...].astype(BF16), preferred_element_type=F32)
    o_ref[...] = y.astype(o_ref.dtype)


def _qkv_kernel(x_ref, g_ref, w_ref, hg_ref, c_ref, sa_ref, sb_ref, o_ref, hn_ref, *, tn, sec_w):
    j = pl.program_id(1)

    @pl.when(j == 0)
    def _():
        hn_ref[...] = _rms(x_ref[...], g_ref[...]).astype(BF16)

    y = jnp.dot(hn_ref[...], w_ref[...].astype(BF16), preferred_element_type=F32)
    sec = j // (sec_w // tn)
    is_v = jnp.logical_or(sec == 2, sec == 5)
    is_mb = jnp.logical_or(sec == 3, sec == 4)
    is_sb = jnp.logical_or(sec == 0, sec == 1)

    def split_dot(t, mat):
        hi = t.astype(BF16)
        lo = (t - hi.astype(F32)).astype(BF16)
        return (jnp.dot(hi, mat, preferred_element_type=F32) + jnp.dot(lo, mat, preferred_element_type=F32))

    def head_norm(hh):
        t = y[:, hh * HEAD_DIM:(hh + 1) * HEAD_DIM]
        mean_mat = jnp.full((HEAD_DIM, HEAD_DIM), 1.0 / HEAD_DIM, BF16)
        ms = jnp.dot((t * t).astype(BF16), mean_mat, preferred_element_type=F32)
        return t * lax.rsqrt(ms + NORM_EPS) * hg_ref[0]

    @pl.when(is_v)
    def _():
        o_ref[...] = y.astype(o_ref.dtype)

    @pl.when(is_sb)
    def _():
        for hh in range(tn // HEAD_DIM):
            o_ref[:, hh * HEAD_DIM:(hh + 1) * HEAD_DIM] = head_norm(hh).astype(o_ref.dtype)

    @pl.when(is_mb)
    def _():
        half = ROPE_DIM // 2
        src = lax.broadcasted_iota(jnp.int32, (HEAD_DIM, 2 * HEAD_DIM), 0)
        dst = lax.broadcasted_iota(jnp.int32, (HEAD_DIM, 2 * HEAD_DIM), 1)
        from_below = jnp.logical_and(dst < HEAD_DIM, src + half == dst)
        from_above = jnp.logical_and(dst >= HEAD_DIM, src - half == dst - HEAD_DIM)
        shift_mat = jnp.where(jnp.logical_or(from_below, from_above), 1.0, 0.0).astype(BF16)
        for hh in range(tn // HEAD_DIM):
            t = head_norm(hh)
            rot = split_dot(t, shift_mat)
            t = t * c_ref[0] + rot[:, :HEAD_DIM] * sa_ref[0] + rot[:, HEAD_DIM:] * sb_ref[0]
            o_ref[:, hh * HEAD_DIM:(hh + 1) * HEAD_DIM] = t.astype(o_ref.dtype)


def _norm_matmul(x, g, w, *, tm, tn, out_dtype=BF16):
    t, d = x.shape
    n = w.shape[1]
    return pl.pallas_call(
        _norm_matmul_kernel,
        out_shape=jax.ShapeDtypeStruct((t, n), out_dtype),
        grid=(t // tm, n // tn),
        in_specs=[
            pl.BlockSpec((tm, d), lambda i, j: (i, 0)),
            pl.BlockSpec((1, d), lambda i, j: (0, 0)),
            pl.BlockSpec((d, tn), lambda i, j: (0, j)),
        ],
        out_specs=pl.BlockSpec((tm, tn), lambda i, j: (i, j)),
        scratch_shapes=[pltpu.VMEM((tm, d), BF16)],
        compiler_params=_params(("parallel", "arbitrary")),
        name="norm_matmul",
    )(x, g.reshape(1, d), w)


def _rope_tables(seq):
    half = ROPE_DIM // 2
    inv_freq = ROPE_THETA ** (-jnp.arange(0, ROPE_DIM, 2, dtype=F32) / ROPE_DIM)
    ang = jnp.arange(seq, dtype=F32)[:, None] * inv_freq[None, :]
    cos, sin = jnp.cos(ang), jnp.sin(ang)
    ones = jnp.ones((seq, HEAD_DIM - ROPE_DIM), F32)
    zeros = jnp.zeros((seq, HEAD_DIM - ROPE_DIM), F32)
    zh = jnp.zeros((seq, half), F32)
    c = jnp.concatenate([cos, cos, ones], axis=1)
    sa = jnp.concatenate([zh, sin, zeros], axis=1)
    sb = jnp.concatenate([-sin, zh, zeros], axis=1)
    ident = jnp.ones((seq, HEAD_DIM), F32)
    z = jnp.zeros((seq, HEAD_DIM), F32)
    return jnp.stack([ident, c]), jnp.stack([z, sa]), jnp.stack([z, sb])


def _qkv_proj(x, g, w, head_gains, seq, *, tm, tn):
    t, d = x.shape
    n = w.shape[1]
    sec_w = n // 6
    c, sa, sb = _rope_tables(seq)
    n_s = seq // tm
    kern = functools.partial(_qkv_kernel, tn=tn, sec_w=sec_w)
    per_sec = sec_w // tn
    return pl.pallas_call(
        kern,
        out_shape=jax.ShapeDtypeStruct((t, n), BF16),
        grid=(t // tm, n // tn),
        in_specs=[
            pl.BlockSpec((tm, d), lambda i, j: (i, 0)),
            pl.BlockSpec((1, d), lambda i, j: (0, 0)),
            pl.BlockSpec((d, tn), lambda i, j: (0, j)),
            pl.BlockSpec((1, 1, HEAD_DIM), lambda i, j: (j // per_sec, 0, 0)),
            pl.BlockSpec((1, tm, HEAD_DIM), lambda i, j: ((j // per_sec) // 3, i % n_s, 0)),
            pl.BlockSpec((1, tm, HEAD_DIM), lambda i, j: ((j // per_sec) // 3, i % n_s, 0)),
            pl.BlockSpec((1, tm, HEAD_DIM), lambda i, j: ((j // per_sec) // 3, i % n_s, 0)),
        ],
        out_specs=pl.BlockSpec((tm, tn), lambda i, j: (i, j)),
        scratch_shapes=[pltpu.VMEM((tm, d), BF16)],
        compiler_params=_params(("parallel", "arbitrary")),
        name="qkv_proj",
    )(x, g.reshape(1, d), w, head_gains, c, sa, sb)


def _matmul_res_kernel(*refs, n_a):
    a_refs = refs[:n_a]
    w_ref, r_ref, o_ref, wb_ref = refs[n_a:]

    @pl.when(pl.program_id(0) == 0)
    def _():
        wb_ref[...] = w_ref[...].astype(BF16)

    acc = r_ref[...]
    k0 = 0
    for a_ref in a_refs:
        k = a_ref.shape[1]
        acc = acc + jnp.dot(a_ref[...], wb_ref[k0:k0 + k, :], preferred_element_type=F32)
        k0 += k
    o_ref[...] = acc


def _matmul_res(a_list, w, res, *, tm):
    t = res.shape[0]
    k, n = w.shape
    kern = functools.partial(_matmul_res_kernel, n_a=len(a_list))
    a_specs = [pl.BlockSpec((tm, a.shape[1]), lambda i: (i, 0)) for a in a_list]
    return pl.pallas_call(
        kern,
        out_shape=jax.ShapeDtypeStruct((t, n), F32),
        grid=(t // tm,),
        in_specs=a_specs + [
            pl.BlockSpec((k, n), lambda i: (0, 0), pipeline_mode=pl.Buffered(1)),
            pl.BlockSpec((tm, n), lambda i: (i, 0)),
        ],
        out_specs=pl.BlockSpec((tm, n), lambda i: (i, 0)),
        scratch_shapes=[pltpu.VMEM((k, n), BF16)],
        compiler_params=_params(("arbitrary",)),
        name="matmul_res",
    )(*a_list, w, res)


def _silu_mul(gate, up):
    return gate * (1.0 / (1.0 + jnp.exp(-gate))) * up


def _ffn_kernel(x_ref, g_ref, wg_ref, wu_ref, wd_ref, o_ref, hn_ref):
    @pl.when(pl.program_id(1) == 0)
    def _():
        x = x_ref[...]
        hn_ref[...] = _rms(x, g_ref[...]).astype(BF16)
        o_ref[...] = x

    hn = hn_ref[...]
    gate = jnp.dot(hn, wg_ref[...].astype(BF16), preferred_element_type=F32)
    up = jnp.dot(hn, wu_ref[...].astype(BF16), preferred_element_type=F32)
    h = _silu_mul(gate, up).astype(BF16)
    o_ref[...] += jnp.dot(h, wd_ref[...].astype(BF16), preferred_element_type=F32)


def _ffn(x, g, wg, wu, wd, *, tm, tf):
    t, d = x.shape
    dff = wg.shape[1]
    return pl.pallas_call(
        _ffn_kernel,
        out_shape=jax.ShapeDtypeStruct((t, d), F32),
        grid=(t // tm, dff // tf),
        in_specs=[
            pl.BlockSpec((tm, d), lambda i, f: (i, 0), pipeline_mode=pl.Buffered(1)),
            pl.BlockSpec((1, d), lambda i, f: (0, 0)),
            pl.BlockSpec((d, tf), lambda i, f: (0, f)),
            pl.BlockSpec((d, tf), lambda i, f: (0, f)),
            pl.BlockSpec((tf, d), lambda i, f: (f, 0)),
        ],
        out_specs=pl.BlockSpec((tm, d), lambda i, f: (i, 0)),
        scratch_shapes=[pltpu.VMEM((tm, d), BF16)],
        compiler_params=_params(("parallel", "arbitrary")),
        name="ffn_swiglu",
    )(x, g.reshape(1, d), wg, wu, wd)


def _transpose_v(v_ref, vt_ref, blk):
    for c in range(v_ref.shape[0] // blk):
        vt_ref[:, c * blk:(c + 1) * blk] = v_ref[c * blk:(c + 1) * blk, :].astype(F32).T.astype(vt_ref.dtype)


SB_DEAD_LOG_WEIGHT = -110.0


def _sb_kernel(q_ref, k_ref, v_ref, o_ref, vt_ref, acc_ref, *, blk):
    qi = pl.program_id(2)

    @pl.when(qi == 0)
    def _():
        _transpose_v(v_ref, vt_ref, blk)

    q = q_ref[...]
    key = lax.broadcasted_iota(jnp.int32, (blk, blk), 0)
    qry = lax.broadcasted_iota(jnp.int32, (blk, blk), 1)
    later_mat = jnp.where(qry > key, 1.0, 0.0).astype(BF16)

    def log_weights(kj, diagonal):
        start = pl.multiple_of(kj * blk, blk)
        z = lax.dot_general(k_ref[pl.ds(start, blk), :], q, (((1,), (1,)), ((), ())),
                            preferred_element_type=F32)
        sp = jnp.log(1.0 + jnp.exp(-jnp.abs(z)))
        log_beta = jnp.minimum(z, 0.0) - sp
        log_keep = log_beta - z
        if diagonal:
            past = key < qry
            log_keep = jnp.where(past, log_keep, 0.0)
        lk_hi = log_keep.astype(BF16)
        lk_lo = (log_keep - lk_hi.astype(F32)).astype(BF16)
        later = (jnp.dot(later_mat, lk_hi, preferred_element_type=F32)
                 + jnp.dot(later_mat, lk_lo, preferred_element_type=F32))
        lw = log_beta + later
        if diagonal:
            lw = jnp.where(past, lw, NEG_INF)
        return start, lw, jnp.sum(log_keep, axis=0, keepdims=True)

    def weighted_values(start, lw, c):
        w = jnp.exp(lw + c).astype(BF16)
        return jnp.dot(vt_ref[:, pl.ds(start, blk)], w, preferred_element_type=F32)

    has_prev = qi > 0
    st0, lw0, sum0 = log_weights(qi, True)
    st1, lw1, sum1 = log_weights(jnp.maximum(qi - 1, 0), False)
    acc_ref[...] = (weighted_values(st0, lw0, jnp.zeros((1, blk), F32))
                    + weighted_values(st1, lw1, jnp.where(has_prev, sum0, NEG_INF)))
    c = sum0 + jnp.where(has_prev, sum1, 0.0)

    def cond(carry):
        kj, _, alive = carry
        return jnp.logical_and(kj >= 0, alive)

    def body(carry):
        kj, c, _ = carry
        st, lw, sm = log_weights(kj, False)
        acc_ref[...] += weighted_values(st, lw, c)
        c = c + sm
        return kj - 1, c, jnp.max(c) > SB_DEAD_LOG_WEIGHT

    lax.while_loop(cond, body, (qi - 2, c, jnp.max(c) > SB_DEAD_LOG_WEIGHT))
    o_ref[...] = acc_ref[...].T.astype(o_ref.dtype)


def _sb_attention(qkv, batch, seq, *, blk):
    t = qkv.shape[0]
    bq = blk
    nq = seq // bq
    kern = functools.partial(_sb_kernel, blk=blk)
    return pl.pallas_call(
        kern,
        out_shape=jax.ShapeDtypeStruct((t, N_HEADS_SB * HEAD_DIM), BF16),
        grid=(batch, N_HEADS_SB, nq),
        in_specs=[
            pl.BlockSpec((bq, HEAD_DIM), lambda b, h, i: (b * nq + i, h)),
            pl.BlockSpec((seq, HEAD_DIM), lambda b, h, i: (b, N_HEADS_SB + h)),
            pl.BlockSpec((seq, HEAD_DIM), lambda b, h, i: (b, 2 * N_HEADS_SB + h)),
        ],
        out_specs=pl.BlockSpec((bq, HEAD_DIM), lambda b, h, i: (b * nq + i, h)),
        scratch_shapes=[pltpu.VMEM((HEAD_DIM, seq), BF16), pltpu.VMEM((HEAD_DIM, bq), F32)],
        compiler_params=_params(("parallel", "parallel", "arbitrary")),
        name="sb_attention",
    )(qkv, qkv, qkv)


MOBA_GROUP = 2


def _moba_kernel(q_ref, k_ref, v_ref, o_ref, vt_ref, kmean_ref, *, blk, seq):
    qi = pl.program_id(2)
    n_blocks = seq // blk

    @pl.when(qi == 0)
    def _():
        _transpose_v(v_ref, vt_ref, blk)
        r = lax.broadcasted_iota(jnp.int32, (n_blocks, seq), 0)
        s = lax.broadcasted_iota(jnp.int32, (n_blocks, seq), 1)
        avg = jnp.where(s // blk == r, 1.0 / blk, 0.0).astype(BF16)
        kmean_ref[...] = jnp.dot(avg, k_ref[...], preferred_element_type=F32)

    q = q_ref[...]
    gate = lax.dot_general(kmean_ref[...], q.astype(F32), (((1,), (1,)), ((), ())),
                           precision=lax.Precision.HIGHEST, preferred_element_type=F32)
    blk_id = lax.broadcasted_iota(jnp.int32, (n_blocks, blk), 0)
    lowest = jnp.float32(-3.0e38)
    is_past = blk_id < qi
    g = jnp.where(is_past, gate, lowest)
    bias = jnp.full((n_blocks, blk), NEG_INF, F32)
    for _ in range(MOBA_TOPK):
        mx = jnp.max(g, axis=0, keepdims=True)
        first = jnp.min(jnp.where(g == mx, blk_id, n_blocks), axis=0, keepdims=True)
        pick = blk_id == first
        bias = jnp.where(jnp.logical_and(pick, is_past), 0.0, bias)
        g = jnp.where(pick, lowest, g)
    bias = jnp.where(blk_id == qi, 0.0, bias)

    key = lax.broadcasted_iota(jnp.int32, (blk, blk), 0)
    qry = lax.broadcasted_iota(jnp.int32, (blk, blk), 1) + qi * blk

    def attend(extent):
        s = lax.dot_general(k_ref[0:extent * blk, :], q, (((1,), (1,)), ((), ())),
                            preferred_element_type=F32)
        parts = []
        for n in range(extent):
            part = s[n * blk:(n + 1) * blk, :] + bias[n:n + 1, :]
            if n >= extent - MOBA_GROUP:
                part = jnp.where(key + n * blk <= qry, part, NEG_INF)
            parts.append(part)
        s = jnp.concatenate(parts, axis=0)
        m = jnp.max(s, axis=0, keepdims=True)
        p = jnp.exp2(s - m)
        l = jnp.sum(p, axis=0, keepdims=True)
        acc = jnp.dot(vt_ref[:, 0:extent * blk], p.astype(BF16), preferred_element_type=F32)
        o_ref[...] = (acc / l).T.astype(o_ref.dtype)

    n_groups = -(-n_blocks // MOBA_GROUP)
    for grp in range(n_groups):
        @pl.when(qi // MOBA_GROUP == grp)
        def _(grp=grp):
            attend(min((grp + 1) * MOBA_GROUP, n_blocks))


def _moba_attention(qkv, batch, seq):
    t = qkv.shape[0]
    blk = MOBA_BLOCK
    nq = seq // blk
    col0 = 3 * N_HEADS_SB
    kern = functools.partial(_moba_kernel, blk=blk, seq=seq)
    return pl.pallas_call(
        kern,
        out_shape=jax.ShapeDtypeStruct((t, N_HEADS_MOBA * HEAD_DIM), BF16),
        grid=(batch, N_HEADS_MOBA, nq),
        in_specs=[
            pl.BlockSpec((blk, HEAD_DIM), lambda b, h, i: (b * nq + i, col0 + h)),
            pl.BlockSpec((seq, HEAD_DIM), lambda b, h, i: (b, col0 + N_HEADS_MOBA + h)),
            pl.BlockSpec((seq, HEAD_DIM), lambda b, h, i: (b, col0 + 2 * N_HEADS_MOBA + h)),
        ],
        out_specs=pl.BlockSpec((blk, HEAD_DIM), lambda b, h, i: (b * nq + i, h)),
        scratch_shapes=[
            pltpu.VMEM((HEAD_DIM, seq), BF16),
            pltpu.VMEM((seq // blk, HEAD_DIM), F32),
        ],
        compiler_params=_params(("parallel", "parallel", "arbitrary")),
        name="moba_attention",
    )(qkv, qkv, qkv)


def _attention_layer(x, seq, batch, attn_norm, w_in, qn_sb, kn_sb, qn_mb, kn_mb, w_o,
                     ffn_norm, w_gate, w_up, w_down, tiles):
    scale = HEAD_DIM ** -0.5
    ones = jnp.ones((HEAD_DIM,), F32)
    head_gains = jnp.stack([qn_sb * scale, kn_sb, ones, qn_mb * (scale * math.log2(math.e)), kn_mb, ones])
    head_gains = head_gains.reshape(6, 1, HEAD_DIM)
    qkv = _qkv_proj(x, attn_norm, w_in, head_gains, seq, tm=tiles.rows, tn=tiles.proj_cols)
    o_sb = _sb_attention(qkv, batch, seq, blk=tiles.sb_block)
    o_mb = _moba_attention(qkv, batch, seq)
    x = _matmul_res([o_sb, o_mb], w_o, x, tm=tiles.out_rows)
    return _ffn(x, ffn_norm, w_gate, w_up, w_down, tm=tiles.rows, tf=tiles.ff_cols)


def _sigmoid(x):
    return 0.5 + 0.5 * jnp.tanh(0.5 * x)


def _gelu_tanh(x):
    return 0.5 * x * (1.0 + jnp.tanh(math.sqrt(2.0 / math.pi) * (x + 0.044715 * (x * x * x))))


def _rglru_kernel(xb_ref, gt_ref, cw_ref, cb_ref, wa_ref, ba_ref, wi_ref, bi_ref, lam_ref, o_ref,
                  tail_ref, h_ref, *, tt):
    ti = pl.program_id(2)
    seg = tt // SUBLANES
    bw = o_ref.shape[1]

    @pl.when(ti == 0)
    def _():
        tail_ref[...] = jnp.zeros_like(tail_ref)
        h_ref[...] = jnp.zeros_like(h_ref)

    def to_slabs(v):
        return pltpu.einshape("jkl->kjl", v.reshape(SUBLANES, seg, bw))

    x3 = to_slabs(xb_ref[...].astype(F32))
    sub = lax.broadcasted_iota(jnp.int32, (SUBLANES, bw), 0)
    halo = [jnp.where(sub == 0, tail_ref[d - 1:d, :], pltpu.roll(x3[seg - d], 1, 0))
            for d in range(1, CONV_WIDTH)]
    for d in range(1, CONV_WIDTH):
        tail_ref[d - 1:d, :] = x3[seg - d][SUBLANES - 1:SUBLANES, :]
    cw = cw_ref[...]
    xc_slabs = []
    for k in range(seg):
        acc = cb_ref[...] + x3[k] * cw[CONV_WIDTH - 1:CONV_WIDTH, :]
        for d in range(1, CONV_WIDTH):
            prev = x3[k - d] if k >= d else halo[d - k - 1]
            acc = acc + prev * cw[CONV_WIDTH - 1 - d:CONV_WIDTH - d, :]
        xc_slabs.append(acc)
    xc = jnp.stack(xc_slabs, axis=0).reshape(tt, bw)

    xcb = xc.astype(BF16)
    r = _sigmoid(jnp.dot(xcb, wa_ref[0].astype(BF16), preferred_element_type=F32) + ba_ref[...])
    ig = _sigmoid(jnp.dot(xcb, wi_ref[0].astype(BF16), preferred_element_type=F32) + bi_ref[...])
    neg_lam = -lam_ref[...]
    sp = jnp.maximum(neg_lam, 0.0) + jnp.log1p(jnp.exp(-jnp.abs(neg_lam)))
    log_a = (-LRU_C * r) * sp
    a = jnp.exp(log_a)
    one_minus_a2 = -jnp.tanh(log_a) * (a * a + 1.0)
    root = jnp.where(one_minus_a2 > 0.0, one_minus_a2 * lax.rsqrt(one_minus_a2), 0.0)
    u = root * (ig * xc)

    a3 = a.reshape(seg, SUBLANES, bw)
    u3 = u.reshape(seg, SUBLANES, bw)
    h_loc = jnp.zeros((SUBLANES, bw), F32)
    prod = jnp.ones((SUBLANES, bw), F32)
    h_slabs, p_slabs = [], []
    for k in range(seg):
        h_loc = a3[k] * h_loc + u3[k]
        prod = a3[k] * prod
        h_slabs.append(h_loc)
        p_slabs.append(prod)
    carry = h_ref[0:1, :]
    carries = []
    for j in range(SUBLANES):
        carries.append(carry)
        carry = prod[j:j + 1, :] * carry + h_loc[j:j + 1, :]
    h_ref[0:1, :] = carry
    c_in = jnp.concatenate(carries, axis=0)
    h3 = jnp.stack([h_slabs[k] + p_slabs[k] * c_in for k in range(seg)], axis=0)
    y3 = h3 * _gelu_tanh(to_slabs(gt_ref[...].astype(F32)))
    o_ref[...] = pltpu.einshape("kjl->jkl", y3).reshape(tt, bw).astype(o_ref.dtype)


def _rglru(xbg, batch, seq, conv_w, conv_b, w_a, b_a, w_i, b_i, lam, *, tt):
    t = xbg.shape[0]
    width = xbg.shape[1] // 2
    bw = width // LRU_BLOCKS
    n_t = seq // tt
    kern = functools.partial(_rglru_kernel, tt=tt)
    vec = lambda a: a.reshape(1, width)
    vspec = pl.BlockSpec((1, bw), lambda b, c, i: (0, c))
    return pl.pallas_call(
        kern,
        out_shape=jax.ShapeDtypeStruct((t, width), BF16),
        grid=(batch, LRU_BLOCKS, n_t),
        in_specs=[
            pl.BlockSpec((tt, bw), lambda b, c, i: (b * n_t + i, c)),
            pl.BlockSpec((tt, bw), lambda b, c, i: (b * n_t + i, LRU_BLOCKS + c)),
            pl.BlockSpec((CONV_WIDTH, bw), lambda b, c, i: (0, c)),
            vspec,
            pl.BlockSpec((1, bw, bw), lambda b, c, i: (c, 0, 0)),
            vspec,
            pl.BlockSpec((1, bw, bw), lambda b, c, i: (c, 0, 0)),
            vspec,
            vspec,
        ],
        out_specs=pl.BlockSpec((tt, bw), lambda b, c, i: (b * n_t + i, c)),
        scratch_shapes=[
            pltpu.VMEM((SUBLANES, bw), F32),
            pltpu.VMEM((SUBLANES, bw), F32),
        ],
        compiler_params=_params(("parallel", "parallel", "arbitrary")),
        name="rglru",
    )(xbg, xbg, conv_w, vec(conv_b), w_a, vec(b_a), w_i, vec(b_i), vec(lam))


def _recurrent_block(x, seq, batch, rec_norm, w_in, conv_w, conv_b, w_a, b_a, w_i, b_i, lam, w_o, tiles):
    xbg = _norm_matmul(x, rec_norm, w_in, tm=tiles.rows, tn=tiles.proj_cols)
    y = _rglru(xbg, batch, seq, conv_w, conv_b, w_a, b_a, w_i, b_i, lam, tt=tiles.scan_rows)
    return _matmul_res([y], w_o, x, tm=tiles.out_rows)


def _router_kernel(x_ref, g_ref, wr_ref, idx_ref, wt_ref, cnt_ref, hp_ref, run_ref, *, rows):
    @pl.when(pl.program_id(0) == 0)
    def _():
        run_ref[...] = jnp.zeros_like(run_ref)

    hn = _rms(x_ref[...], g_ref[...])
    bits = lax.bitcast_convert_type(hn.astype(BF16).astype(F32), jnp.uint32)
    half = bits.shape[1] // 2
    words = jnp.bitwise_or(jnp.bitwise_and(bits[:, half:], jnp.uint32(0xFFFF0000)),
                           jnp.right_shift(bits[:, :half], jnp.uint32(16)))
    for c in range(hp_ref.shape[1]):
        hp_ref[:, c, :] = words[:, c * LANES:(c + 1) * LANES]
    logits = jnp.dot(hn, wr_ref[...], precision=lax.Precision.HIGHEST, preferred_element_type=F32)
    lane = lax.broadcasted_iota(jnp.int32, (rows, LANES), 1)
    lowest = jnp.float32(-3.0e38)
    logits = jnp.where(lane < N_EXPERTS, logits, lowest)
    m1 = jnp.max(logits, axis=1, keepdims=True)
    e1 = jnp.min(jnp.where(logits == m1, lane, LANES), axis=1, keepdims=True)
    rest = jnp.where(lane == e1, lowest, logits)
    m2 = jnp.max(rest, axis=1, keepdims=True)
    e2 = jnp.min(jnp.where(rest == m2, lane, LANES), axis=1, keepdims=True)
    ex = jnp.exp(m2 - m1)
    w1 = 1.0 / (1.0 + ex)
    w2 = ex / (1.0 + ex)

    onehot = jnp.where(jnp.logical_or(lane == e1, lane == e2), 1.0, 0.0)
    rr = lax.broadcasted_iota(jnp.int32, (rows, rows), 0)
    cc = lax.broadcasted_iota(jnp.int32, (rows, rows), 1)
    before = jnp.where(cc < rr, 1.0, 0.0).astype(BF16)
    rank = jnp.dot(before, onehot.astype(BF16), preferred_element_type=F32) + run_ref[0:1, :]
    r1 = jnp.sum(jnp.where(lane == e1, rank, 0.0), axis=1, keepdims=True).astype(jnp.int32)
    r2 = jnp.sum(jnp.where(lane == e2, rank, 0.0), axis=1, keepdims=True).astype(jnp.int32)
    run_ref[0:1, :] = run_ref[0:1, :] + jnp.sum(onehot, axis=0, keepdims=True)

    packed = jnp.where(lane == 0, e1, jnp.where(lane == 1, e2, jnp.where(lane == 2, r1, r2)))
    idx_ref[...] = packed[:, 0:idx_ref.shape[1]]
    wts = jnp.where(lane == 0, w1, w2)
    wt_ref[...] = wts[:, 0:wt_ref.shape[1]]
    cnt_ref[...] = run_ref[...].astype(jnp.int32)


def _router(x, g, router, *, rows):
    t, d = x.shape
    wr = jnp.pad(router, ((0, 0), (0, LANES - router.shape[1])))
    kern = functools.partial(_router_kernel, rows=rows)
    return pl.pallas_call(
        kern,
        out_shape=(
            jax.ShapeDtypeStruct((t, SUBLANES), jnp.int32),
            jax.ShapeDtypeStruct((t, SUBLANES), F32),
            jax.ShapeDtypeStruct((SUBLANES, LANES), jnp.int32),
            jax.ShapeDtypeStruct((t, d // 2 // LANES, LANES), jnp.uint32),
        ),
        grid=(t // rows,),
        in_specs=[
            pl.BlockSpec((rows, d), lambda i: (i, 0)),
            pl.BlockSpec((1, d), lambda i: (0, 0)),
            pl.BlockSpec((d, LANES), lambda i: (0, 0)),
        ],
        out_specs=(
            pl.BlockSpec((rows, SUBLANES), lambda i: (i, 0)),
            pl.BlockSpec((rows, SUBLANES), lambda i: (i, 0)),
            pl.BlockSpec((SUBLANES, LANES), lambda i: (0, 0)),
            pl.BlockSpec((rows, d // 2 // LANES, LANES), lambda i: (i, 0, 0)),
        ),
        scratch_shapes=[pltpu.VMEM((SUBLANES, LANES), F32)],
        compiler_params=_params(("arbitrary",)),
        name="moe_router",
    )(x, g.reshape(1, d), wr)


MOE_SUB_ROWS = 256


def _moe_ffn_kernel(te_ref, nu_ref, tv_ref, tok_ref, tokn_ref, hp_ref, wg_ref, wu_ref, wd_ref, o_ref,
                    xg_ref, hn_ref, sem, *, tm, n_f):
    del te_ref
    i = pl.program_id(0)
    f = pl.program_id(1)
    n_used = nu_ref[0]
    slot = lax.rem(i, 2)
    per_step = tm // n_f
    extra = tm - per_step * n_f

    def row_copy(tok, r, s):
        return pltpu.make_async_copy(hp_ref.at[tok[r]], xg_ref.at[s, r], sem.at[s])

    @pl.when(jnp.logical_and(i >= n_used, f == 0))
    def _():
        o_ref[...] = jnp.zeros_like(o_ref)

    @pl.when(i < n_used)
    def _():
        @pl.when(f == 0)
        def _():
            @pl.when(i == 0)
            def _():
                def start(r, carry):
                    row_copy(tok_ref, r, 0).start()
                    return carry
                lax.fori_loop(0, tm, start, 0)

            pltpu.make_async_copy(hp_ref.at[pl.ds(0, tm)], xg_ref.at[slot], sem.at[slot]).wait()
            chunks = pltpu.einshape("tcl->ctl", xg_ref[slot])
            half = hn_ref.shape[1] // 2
            for c in range(chunks.shape[0]):
                words = chunks[c]
                lo = lax.bitcast_convert_type(jnp.left_shift(words, jnp.uint32(16)), F32)
                hi = lax.bitcast_convert_type(jnp.bitwise_and(words, jnp.uint32(0xFFFF0000)), F32)
                hn_ref[:, c * LANES:(c + 1) * LANES] = lo.astype(BF16)
                hn_ref[:, half + c * LANES:half + (c + 1) * LANES] = hi.astype(BF16)
            o_ref[...] = jnp.zeros_like(o_ref)

        @pl.when(i + 1 < n_used)
        def _():
            base = f * per_step + jnp.minimum(f, extra)
            for rr in range(per_step):
                row_copy(tokn_ref, base + rr, 1 - slot).start()
            if extra:
                @pl.when(f < extra)
                def _():
                    row_copy(tokn_ref, base + per_step, 1 - slot).start()

        n_sub = lax.shift_right_logical(tv_ref[i] + (MOE_SUB_ROWS - 1), MOE_SUB_ROWS.bit_length() - 1)
        for k in range(1, tm // MOE_SUB_ROWS + 1):
            @pl.when(n_sub == k)
            def _(k=k):
                rows = k * MOE_SUB_ROWS
                hn = hn_ref[0:rows, :]
                gate = jnp.dot(hn, wg_ref[0].astype(BF16), preferred_element_type=F32)
                up = jnp.dot(hn, wu_ref[0].astype(BF16), preferred_element_type=F32)
                h = _silu_mul(gate, up).astype(BF16)
                o_ref[0:rows, :] += jnp.dot(h, wd_ref[0].astype(BF16), preferred_element_type=F32)


def _moe_ffn(hp, tok, we_gate, we_up, we_down, tile_expert, n_used, tile_valid, *, tm, tf):
    d = 2 * hp.shape[1] * hp.shape[2]
    n_rows = tok.shape[0]
    dff = we_gate.shape[2]
    n_f = dff // tf
    n_tiles = n_rows // tm
    assert tm % MOE_SUB_ROWS == 0 and MOE_SUB_ROWS & (MOE_SUB_ROWS - 1) == 0

    def tile(i, nu):
        return jnp.maximum(jnp.minimum(i, nu[0] - 1), 0)

    def fidx(i, f, nu):
        return jnp.where(i < nu[0], f, n_f - 1)

    grid_spec = pltpu.PrefetchScalarGridSpec(
        num_scalar_prefetch=3,
        grid=(n_tiles, n_f),
        in_specs=[
            pl.BlockSpec((tm,), lambda i, f, te, nu, tv: (tile(i, nu),), memory_space=pltpu.SMEM),
            pl.BlockSpec((tm,), lambda i, f, te, nu, tv: (tile(i + 1, nu),), memory_space=pltpu.SMEM),
            pl.BlockSpec(memory_space=pl.ANY),
            pl.BlockSpec((1, d, tf), lambda i, f, te, nu, tv: (te[tile(i, nu)], 0, fidx(i, f, nu))),
            pl.BlockSpec((1, d, tf), lambda i, f, te, nu, tv: (te[tile(i, nu)], 0, fidx(i, f, nu))),
            pl.BlockSpec((1, tf, d), lambda i, f, te, nu, tv: (te[tile(i, nu)], fidx(i, f, nu), 0)),
        ],
        out_specs=pl.BlockSpec((tm, d), lambda i, f, te, nu, tv: (i, 0)),
        scratch_shapes=[
            pltpu.VMEM((2, tm) + hp.shape[1:], jnp.uint32),
            pltpu.VMEM((tm, d), BF16),
            pltpu.SemaphoreType.DMA((2,)),
        ],
    )
    return pl.pallas_call(
        functools.partial(_moe_ffn_kernel, tm=tm, n_f=n_f),
        out_shape=jax.ShapeDtypeStruct((n_rows, d), F32),
        grid_spec=grid_spec,
        compiler_params=_params(("arbitrary", "arbitrary")),
        name="moe_ffn",
    )(tile_expert, n_used, tile_valid, tok, tok, hp, we_gate, we_up, we_down)


COMBINE_ROWS = 512


def _combine_kernel(pos_ref, x_ref, wt_ref, ys_ref, o_ref, y1_ref, y2_ref, sem):
    def row_copy(r, k, dst):
        return pltpu.make_async_copy(ys_ref.at[pl.ds(pos_ref[2 * r + k], 1)], dst.at[pl.ds(r, 1)], sem)

    def start(r, carry):
        row_copy(r, 0, y1_ref).start()
        row_copy(r, 1, y2_ref).start()
        return carry

    lax.fori_loop(0, COMBINE_ROWS, start, 0)
    for dst in (y1_ref, y2_ref):
        pltpu.make_async_copy(ys_ref.at[pl.ds(0, COMBINE_ROWS)], dst, sem).wait()
    wt = wt_ref[...]
    o_ref[...] = x_ref[...] + wt[:, 0:1] * y1_ref[...] + wt[:, 1:2] * y2_ref[...]


def _combine(x, wts, ys, pos_flat):
    t, d = x.shape
    return pl.pallas_call(
        _combine_kernel,
        out_shape=jax.ShapeDtypeStruct((t, d), F32),
        grid=(t // COMBINE_ROWS,),
        in_specs=[
            pl.BlockSpec((2 * COMBINE_ROWS,), lambda i: (i,), memory_space=pltpu.SMEM),
            pl.BlockSpec((COMBINE_ROWS, d), lambda i: (i, 0)),
            pl.BlockSpec((COMBINE_ROWS, SUBLANES), lambda i: (i, 0)),
            pl.BlockSpec(memory_space=pl.ANY),
        ],
        out_specs=pl.BlockSpec((COMBINE_ROWS, d), lambda i: (i, 0)),
        scratch_shapes=[
            pltpu.VMEM((COMBINE_ROWS, d), F32),
            pltpu.VMEM((COMBINE_ROWS, d), F32),
            pltpu.SemaphoreType.DMA(()),
        ],
        compiler_params=_params(("arbitrary",)),
        name="moe_combine",
    )(pos_flat, x, wts, ys)


def _moe_block(x, ffn_norm, router, we_gate, we_up, we_down, tiles):
    t, d = x.shape
    tm, tf = tiles.rows, tiles.ff_cols
    idx, wts, cnt, hp = _router(x, ffn_norm, router, rows=tiles.route_rows)
    n_tiles = (TOP_K * t) // tm + N_EXPERTS
    counts = cnt[0, :N_EXPERTS]
    padded = ((counts + tm - 1) // tm) * tm
    ends = jnp.cumsum(padded)
    offsets = ends - padded
    pos = jnp.take(offsets, idx[:, 0:2]) + idx[:, 2:4]
    pos_flat = pos.reshape(-1).astype(jnp.int32)
    tile_start = jnp.arange(n_tiles, dtype=jnp.int32) * tm
    tile_expert = jnp.minimum(jnp.sum(tile_start[:, None] >= ends[None, :], axis=1), N_EXPERTS - 1).astype(jnp.int32)
    n_used = (ends[-1:] // tm).astype(jnp.int32)
    tile_valid = jnp.clip(jnp.take(offsets + counts, tile_expert) - tile_start, 0, tm).astype(jnp.int32)
    tok = jnp.zeros((n_tiles * tm,), jnp.int32).at[pos_flat].set(jnp.repeat(jnp.arange(t, dtype=jnp.int32), TOP_K))
    ys = _moe_ffn(hp, tok, we_gate, we_up, we_down, tile_expert, n_used, tile_valid, tm=tm, tf=tf)
    return _combine(x, wts, ys, pos_flat)


def kernel(x, ev_attn_norm, ev_w_in, ev_q_norm_sb, ev_k_norm_sb, ev_q_norm_moba, ev_k_norm_moba, ev_w_o, ev_ffn_norm, ev_w_gate, ev_w_up, ev_w_down, od_rec_norm, od_w_in, od_conv_w, od_conv_b, od_w_a, od_b_a, od_w_i, od_b_i, od_lambda, od_w_o, od_ffn_norm, od_router, od_we_gate, od_we_up, od_we_down):
    batch, seq, d = x.shape
    t = batch * seq
    tiles = _tiles(seq)
    h = x.reshape(t, d)
    h = _attention_layer(h, seq, batch, ev_attn_norm[0], ev_w_in[0], ev_q_norm_sb[0], ev_k_norm_sb[0],
                         ev_q_norm_moba[0], ev_k_norm_moba[0], ev_w_o[0], ev_ffn_norm[0],
                         ev_w_gate[0], ev_w_up[0], ev_w_down[0], tiles)
    h = _recurrent_block(h, seq, batch, od_rec_norm[0], od_w_in[0], od_conv_w[0], od_conv_b[0], od_w_a[0],
                         od_b_a[0], od_w_i[0], od_b_i[0], od_lambda[0], od_w_o[0], tiles)
    h = _moe_block(h, od_ffn_norm[0], od_router[0], od_we_gate[0], od_we_up[0], od_we_down[0], tiles)
    return h.reshape(batch, seq, d)
```

```python
import functools
import math
from typing import NamedTuple

import jax
import jax.numpy as jnp
from jax import lax
from jax.experimental import pallas as pl
from jax.experimental.pallas import tpu as pltpu

F32 = jnp.float32
BF16 = jnp.bfloat16

HEAD_DIM = 128
N_HEADS_SB = 8
N_HEADS_MOBA = 8
ROPE_DIM = HEAD_DIM // 4
ROPE_THETA = 500000.0
MOBA_BLOCK = 256
MOBA_TOPK = 3
LRU_BLOCKS = 8
LRU_C = 8.0
CONV_WIDTH = 4
N_EXPERTS = 8
TOP_K = 2
NORM_EPS = 1e-6
NEG_INF = -1e30

LANES = 128
SUBLANES = 8
VMEM_LIMIT_BYTES = 56 * 1024 * 1024


class _Tiles(NamedTuple):
    rows: int
    out_rows: int
    proj_cols: int
    ff_cols: int
    sb_block: int
    scan_rows: int
    route_rows: int


def _tiles(seq):
    rows = min(1024, seq)
    return _Tiles(rows=rows, out_rows=rows // 2, proj_cols=512, ff_cols=256, sb_block=256,
                  scan_rows=min(512, seq), route_rows=256)


def _params(semantics, vmem=VMEM_LIMIT_BYTES):
    return pltpu.CompilerParams(dimension_semantics=semantics, vmem_limit_bytes=vmem)


def _rms(x, g):
    ms = jnp.mean(x * x, axis=-1, keepdims=True)
    return x * lax.rsqrt(ms + NORM_EPS) * g


def _norm_matmul_kernel(x_ref, g_ref, w_ref, o_ref, hn_ref):
    @pl.when(pl.program_id(1) == 0)
    def _():
        hn_ref[...] = _rms(x_ref[...], g_ref[...]).astype(BF16)

    y = jnp.dot(hn_ref[...], w_ref[...].astype(BF16), preferred_element_type=F32)
    o_ref[...] = y.astype(o_ref.dtype)


def _qkv_kernel(x_ref, g_ref, w_ref, hg_ref, c_ref, sa_ref, sb_ref, o_ref, hn_ref, *, tn, sec_w):
    j = pl.program_id(1)

    @pl.when(j == 0)
    def _():
        hn_ref[...] = _rms(x_ref[...], g_ref[...]).astype(BF16)

    y = jnp.dot(hn_ref[...], w_ref[...].astype(BF16), preferred_element_type=F32)
    sec = j // (sec_w // tn)
    is_v = jnp.logical_or(sec == 2, sec == 5)
    is_mb = jnp.logical_or(sec == 3, sec == 4)
    is_sb = jnp.logical_or(sec == 0, sec == 1)

    def split_dot(t, mat):
        hi = t.astype(BF16)
        lo = (t - hi.astype(F32)).astype(BF16)
        return (jnp.dot(hi, mat, preferred_element_type=F32) + jnp.dot(lo, mat, preferred_element_type=F32))

    def head_norm(hh):
        t = y[:, hh * HEAD_DIM:(hh + 1) * HEAD_DIM]
        mean_mat = jnp.full((HEAD_DIM, HEAD_DIM), 1.0 / HEAD_DIM, BF16)
        ms = jnp.dot((t * t).astype(BF16), mean_mat, preferred_element_type=F32)
        return t * lax.rsqrt(ms + NORM_EPS) * hg_ref[0]

    @pl.when(is_v)
    def _():
        o_ref[...] = y.astype(o_ref.dtype)

    @pl.when(is_sb)
    def _():
        for hh in range(tn // HEAD_DIM):
            o_ref[:, hh * HEAD_DIM:(hh + 1) * HEAD_DIM] = head_norm(hh).astype(o_ref.dtype)

    @pl.when(is_mb)
    def _():
        half = ROPE_DIM // 2
        src = lax.broadcasted_iota(jnp.int32, (HEAD_DIM, 2 * HEAD_DIM), 0)
        dst = lax.broadcasted_iota(jnp.int32, (HEAD_DIM, 2 * HEAD_DIM), 1)
        from_below = jnp.logical_and(dst < HEAD_DIM, src + half == dst)
        from_above = jnp.logical_and(dst >= HEAD_DIM, src - half == dst - HEAD_DIM)
        shift_mat = jnp.where(jnp.logical_or(from_below, from_above), 1.0, 0.0).astype(BF16)
        for hh in range(tn // HEAD_DIM):
            t = head_norm(hh)
            rot = split_dot(t, shift_mat)
            t = t * c_ref[0] + rot[:, :HEAD_DIM] * sa_ref[0] + rot[:, HEAD_DIM:] * sb_ref[0]
            o_ref[:, hh * HEAD_DIM:(hh + 1) * HEAD_DIM] = t.astype(o_ref.dtype)


def _norm_matmul(x, g, w, *, tm, tn, out_dtype=BF16):
    t, d = x.shape
    n = w.shape[1]
    return pl.pallas_call(
        _norm_matmul_kernel,
        out_shape=jax.ShapeDtypeStruct((t, n), out_dtype),
        grid=(t // tm, n // tn),
        in_specs=[
            pl.BlockSpec((tm, d), lambda i, j: (i, 0)),
            pl.BlockSpec((1, d), lambda i, j: (0, 0)),
            pl.BlockSpec((d, tn), lambda i, j: (0, j)),
        ],
        out_specs=pl.BlockSpec((tm, tn), lambda i, j: (i, j)),
        scratch_shapes=[pltpu.VMEM((tm, d), BF16)],
        compiler_params=_params(("parallel", "arbitrary")),
        name="norm_matmul",
    )(x, g.reshape(1, d), w)


def _rope_tables(seq):
    half = ROPE_DIM // 2
    inv_freq = ROPE_THETA ** (-jnp.arange(0, ROPE_DIM, 2, dtype=F32) / ROPE_DIM)
    ang = jnp.arange(seq, dtype=F32)[:, None] * inv_freq[None, :]
    cos, sin = jnp.cos(ang), jnp.sin(ang)
    ones = jnp.ones((seq, HEAD_DIM - ROPE_DIM), F32)
    zeros = jnp.zeros((seq, HEAD_DIM - ROPE_DIM), F32)
    zh = jnp.zeros((seq, half), F32)
    c = jnp.concatenate([cos, cos, ones], axis=1)
    sa = jnp.concatenate([zh, sin, zeros], axis=1)
    sb = jnp.concatenate([-sin, zh, zeros], axis=1)
    ident = jnp.ones((seq, HEAD_DIM), F32)
    z = jnp.zeros((seq, HEAD_DIM), F32)
    return jnp.stack([ident, c]), jnp.stack([z, sa]), jnp.stack([z, sb])


def _qkv_proj(x, g, w, head_gains, seq, *, tm, tn):
    t, d = x.shape
    n = w.shape[1]
    sec_w = n // 6
    c, sa, sb = _rope_tables(seq)
    n_s = seq // tm
    kern = functools.partial(_qkv_kernel, tn=tn, sec_w=sec_w)
    per_sec = sec_w // tn
    return pl.pallas_call(
        kern,
        out_shape=jax.ShapeDtypeStruct((t, n), BF16),
        grid=(t // tm, n // tn),
        in_specs=[
            pl.BlockSpec((tm, d), lambda i, j: (i, 0)),
            pl.BlockSpec((1, d), lambda i, j: (0, 0)),
            pl.BlockSpec((d, tn), lambda i, j: (0, j)),
            pl.BlockSpec((1, 1, HEAD_DIM), lambda i, j: (j // per_sec, 0, 0)),
            pl.BlockSpec((1, tm, HEAD_DIM), lambda i, j: ((j // per_sec) // 3, i % n_s, 0)),
            pl.BlockSpec((1, tm, HEAD_DIM), lambda i, j: ((j // per_sec) // 3, i % n_s, 0)),
            pl.BlockSpec((1, tm, HEAD_DIM), lambda i, j: ((j // per_sec) // 3, i % n_s, 0)),
        ],
        out_specs=pl.BlockSpec((tm, tn), lambda i, j: (i, j)),
        scratch_shapes=[pltpu.VMEM((tm, d), BF16)],
        compiler_params=_params(("parallel", "arbitrary")),
        name="qkv_proj",
    )(x, g.reshape(1, d), w, head_gains, c, sa, sb)


def _matmul_res_kernel(*refs, n_a):
    a_refs = refs[:n_a]
    w_ref, r_ref, o_ref, wb_ref = refs[n_a:]

    @pl.when(pl.program_id(0) == 0)
    def _():
        wb_ref[...] = w_ref[...].astype(BF16)

    acc = r_ref[...]
    k0 = 0
    for a_ref in a_refs:
        k = a_ref.shape[1]
        acc = acc + jnp.dot(a_ref[...], wb_ref[k0:k0 + k, :], preferred_element_type=F32)
        k0 += k
    o_ref[...] = acc


def _matmul_res(a_list, w, res, *, tm):
    t = res.shape[0]
    k, n = w.shape
    kern = functools.partial(_matmul_res_kernel, n_a=len(a_list))
    a_specs = [pl.BlockSpec((tm, a.shape[1]), lambda i: (i, 0)) for a in a_list]
    return pl.pallas_call(
        kern,
        out_shape=jax.ShapeDtypeStruct((t, n), F32),
        grid=(t // tm,),
        in_specs=a_specs + [
            pl.BlockSpec((k, n), lambda i: (0, 0), pipeline_mode=pl.Buffered(1)),
            pl.BlockSpec((tm, n), lambda i: (i, 0)),
        ],
        out_specs=pl.BlockSpec((tm, n), lambda i: (i, 0)),
        scratch_shapes=[pltpu.VMEM((k, n), BF16)],
        compiler_params=_params(("arbitrary",)),
        name="matmul_res",
    )(*a_list, w, res)


def _silu_mul(gate, up):
    return gate * (1.0 / (1.0 + jnp.exp(-gate))) * up


def _ffn_kernel(x_ref, g_ref, wg_ref, wu_ref, wd_ref, o_ref, hn_ref):
    @pl.when(pl.program_id(1) == 0)
    def _():
        x = x_ref[...]
        hn_ref[...] = _rms(x, g_ref[...]).astype(BF16)
        o_ref[...] = x

    hn = hn_ref[...]
    gate = jnp.dot(hn, wg_ref[...].astype(BF16), preferred_element_type=F32)
    up = jnp.dot(hn, wu_ref[...].astype(BF16), preferred_element_type=F32)
    h = _silu_mul(gate, up).astype(BF16)
    o_ref[...] += jnp.dot(h, wd_ref[...].astype(BF16), preferred_element_type=F32)


def _ffn(x, g, wg, wu, wd, *, tm, tf):
    t, d = x.shape
    dff = wg.shape[1]
    return pl.pallas_call(
        _ffn_kernel,
        out_shape=jax.ShapeDtypeStruct((t, d), F32),
        grid=(t // tm, dff // tf),
        in_specs=[
            pl.BlockSpec((tm, d), lambda i, f: (i, 0), pipeline_mode=pl.Buffered(1)),
            pl.BlockSpec((1, d), lambda i, f: (0, 0)),
            pl.BlockSpec((d, tf), lambda i, f: (0, f)),
            pl.BlockSpec((d, tf), lambda i, f: (0, f)),
            pl.BlockSpec((tf, d), lambda i, f: (f, 0)),
        ],
        out_specs=pl.BlockSpec((tm, d), lambda i, f: (i, 0)),
        scratch_shapes=[pltpu.VMEM((tm, d), BF16)],
        compiler_params=_params(("parallel", "arbitrary")),
        name="ffn_swiglu",
    )(x, g.reshape(1, d), wg, wu, wd)


def _transpose_v(v_ref, vt_ref, blk):
    for c in range(v_ref.shape[0] // blk):
        vt_ref[:, c * blk:(c + 1) * blk] = v_ref[c * blk:(c + 1) * blk, :].astype(F32).T.astype(vt_ref.dtype)


SB_DEAD_LOG_WEIGHT = -110.0


def _sb_kernel(q_ref, k_ref, v_ref, o_ref, vt_ref, acc_ref, *, blk):
    qi = pl.program_id(2)

    @pl.when(qi == 0)
    def _():
        _transpose_v(v_ref, vt_ref, blk)

    q = q_ref[...]
    key = lax.broadcasted_iota(jnp.int32, (blk, blk), 0)
    qry = lax.broadcasted_iota(jnp.int32, (blk, blk), 1)
    later_mat = jnp.where(qry > key, 1.0, 0.0).astype(BF16)

    def log_weights(kj, diagonal):
        start = pl.multiple_of(kj * blk, blk)
        z = lax.dot_general(k_ref[pl.ds(start, blk), :], q, (((1,), (1,)), ((), ())),
                            preferred_element_type=F32)
        sp = jnp.log(1.0 + jnp.exp(-jnp.abs(z)))
        log_beta = jnp.minimum(z, 0.0) - sp
        log_keep = log_beta - z
        if diagonal:
            past = key < qry
            log_keep = jnp.where(past, log_keep, 0.0)
        lk_hi = log_keep.astype(BF16)
        lk_lo = (log_keep - lk_hi.astype(F32)).astype(BF16)
        later = (jnp.dot(later_mat, lk_hi, preferred_element_type=F32)
                 + jnp.dot(later_mat, lk_lo, preferred_element_type=F32))
        lw = log_beta + later
        if diagonal:
            lw = jnp.where(past, lw, NEG_INF)
        return start, lw, jnp.sum(log_keep, axis=0, keepdims=True)

    def weighted_values(start, lw, c):
        w = jnp.exp(lw + c).astype(BF16)
        return jnp.dot(vt_ref[:, pl.ds(start, blk)], w, preferred_element_type=F32)

    has_prev = qi > 0
    st0, lw0, sum0 = log_weights(qi, True)
    st1, lw1, sum1 = log_weights(jnp.maximum(qi - 1, 0), False)
    acc_ref[...] = (weighted_values(st0, lw0, jnp.zeros((1, blk), F32))
                    + weighted_values(st1, lw1, jnp.where(has_prev, sum0, NEG_INF)))
    c = sum0 + jnp.where(has_prev, sum1, 0.0)

    def cond(carry):
        kj, _, alive = carry
        return jnp.logical_and(kj >= 0, alive)

    def body(carry):
        kj, c, _ = carry
        st, lw, sm = log_weights(kj, False)
        acc_ref[...] += weighted_values(st, lw, c)
        c = c + sm
        return kj - 1, c, jnp.max(c) > SB_DEAD_LOG_WEIGHT

    lax.while_loop(cond, body, (qi - 2, c, jnp.max(c) > SB_DEAD_LOG_WEIGHT))
    o_ref[...] = acc_ref[...].T.astype(o_ref.dtype)


def _sb_attention(qkv, batch, seq, *, blk):
    t = qkv.shape[0]
    bq = blk
    nq = seq // bq
    kern = functools.partial(_sb_kernel, blk=blk)
    return pl.pallas_call(
        kern,
        out_shape=jax.ShapeDtypeStruct((t, N_HEADS_SB * HEAD_DIM), BF16),
        grid=(batch, N_HEADS_SB, nq),
        in_specs=[
            pl.BlockSpec((bq, HEAD_DIM), lambda b, h, i: (b * nq + i, h)),
            pl.BlockSpec((seq, HEAD_DIM), lambda b, h, i: (b, N_HEADS_SB + h)),
            pl.BlockSpec((seq, HEAD_DIM), lambda b, h, i: (b, 2 * N_HEADS_SB + h)),
        ],
        out_specs=pl.BlockSpec((bq, HEAD_DIM), lambda b, h, i: (b * nq + i, h)),
        scratch_shapes=[pltpu.VMEM((HEAD_DIM, seq), BF16), pltpu.VMEM((HEAD_DIM, bq), F32)],
        compiler_params=_params(("parallel", "parallel", "arbitrary")),
        name="sb_attention",
    )(qkv, qkv, qkv)


MOBA_GROUP = 2


def _moba_kernel(q_ref, k_ref, v_ref, o_ref, vt_ref, kmean_ref, *, blk, seq):
    qi = pl.program_id(2)
    n_blocks = seq // blk

    @pl.when(qi == 0)
    def _():
        _transpose_v(v_ref, vt_ref, blk)
        r = lax.broadcasted_iota(jnp.int32, (n_blocks, seq), 0)
        s = lax.broadcasted_iota(jnp.int32, (n_blocks, seq), 1)
        avg = jnp.where(s // blk == r, 1.0 / blk, 0.0).astype(BF16)
        kmean_ref[...] = jnp.dot(avg, k_ref[...], preferred_element_type=F32)

    q = q_ref[...]
    kmean = kmean_ref[...]
    km1 = kmean.astype(BF16)
    rest = kmean - km1.astype(F32)
    km2 = rest.astype(BF16)
    km3 = (rest - km2.astype(F32)).astype(BF16)
    gate = lax.dot_general(jnp.concatenate([km1, km2, km3], axis=1), jnp.concatenate([q, q, q], axis=1),
                           (((1,), (1,)), ((), ())), preferred_element_type=F32)
    blk_id = lax.broadcasted_iota(jnp.int32, (n_blocks, blk), 0)
    lowest = jnp.float32(-3.0e38)
    is_past = blk_id < qi
    g = jnp.where(is_past, gate, lowest)
    bias = jnp.full((n_blocks, blk), NEG_INF, F32)
    for _ in range(MOBA_TOPK):
        mx = jnp.max(g, axis=0, keepdims=True)
        first = jnp.min(jnp.where(g == mx, blk_id, n_blocks), axis=0, keepdims=True)
        pick = blk_id == first
        bias = jnp.where(jnp.logical_and(pick, is_past), 0.0, bias)
        g = jnp.where(pick, lowest, g)
    bias = jnp.where(blk_id == qi, 0.0, bias)

    key = lax.broadcasted_iota(jnp.int32, (blk, blk), 0)
    qry = lax.broadcasted_iota(jnp.int32, (blk, blk), 1) + qi * blk

    def attend(extent):
        s = lax.dot_general(k_ref[0:extent * blk, :], q, (((1,), (1,)), ((), ())),
                            preferred_element_type=F32)
        parts = []
        for n in range(extent):
            part = s[n * blk:(n + 1) * blk, :] + bias[n:n + 1, :]
            if n >= extent - MOBA_GROUP:
                part = jnp.where(key + n * blk <= qry, part, NEG_INF)
            parts.append(part)
        s = jnp.concatenate(parts, axis=0)
        m = jnp.max(s, axis=0, keepdims=True)
        p = jnp.exp2(s - m)
        l = jnp.sum(p, axis=0, keepdims=True)
        acc = jnp.dot(vt_ref[:, 0:extent * blk], p.astype(BF16), preferred_element_type=F32)
        o_ref[...] = (acc / l).T.astype(o_ref.dtype)

    n_groups = -(-n_blocks // MOBA_GROUP)
    for grp in range(n_groups):
        @pl.when(qi // MOBA_GROUP == grp)
        def _(grp=grp):
            attend(min((grp + 1) * MOBA_GROUP, n_blocks))


def _moba_attention(qkv, batch, seq):
    t = qkv.shape[0]
    blk = MOBA_BLOCK
    nq = seq // blk
    col0 = 3 * N_HEADS_SB
    kern = functools.partial(_moba_kernel, blk=blk, seq=seq)
    return pl.pallas_call(
        kern,
        out_shape=jax.ShapeDtypeStruct((t, N_HEADS_MOBA * HEAD_DIM), BF16),
        grid=(batch, N_HEADS_MOBA, nq),
        in_specs=[
            pl.BlockSpec((blk, HEAD_DIM), lambda b, h, i: (b * nq + i, col0 + h)),
            pl.BlockSpec((seq, HEAD_DIM), lambda b, h, i: (b, col0 + N_HEADS_MOBA + h)),
            pl.BlockSpec((seq, HEAD_DIM), lambda b, h, i: (b, col0 + 2 * N_HEADS_MOBA + h)),
        ],
        out_specs=pl.BlockSpec((blk, HEAD_DIM), lambda b, h, i: (b * nq + i, h)),
        scratch_shapes=[
            pltpu.VMEM((HEAD_DIM, seq), BF16),
            pltpu.VMEM((seq // blk, HEAD_DIM), F32),
        ],
        compiler_params=_params(("parallel", "parallel", "arbitrary")),
        name="moba_attention",
    )(qkv, qkv, qkv)


def _attention_layer(x, seq, batch, attn_norm, w_in, qn_sb, kn_sb, qn_mb, kn_mb, w_o,
                     ffn_norm, w_gate, w_up, w_down, tiles):
    scale = HEAD_DIM ** -0.5
    ones = jnp.ones((HEAD_DIM,), F32)
    head_gains = jnp.stack([qn_sb * scale, kn_sb, ones, qn_mb * (scale * math.log2(math.e)), kn_mb, ones])
    head_gains = head_gains.reshape(6, 1, HEAD_DIM)
    qkv = _qkv_proj(x, attn_norm, w_in, head_gains, seq, tm=tiles.rows, tn=tiles.proj_cols)
    o_sb = _sb_attention(qkv, batch, seq, blk=tiles.sb_block)
    o_mb = _moba_attention(qkv, batch, seq)
    x = _matmul_res([o_sb, o_mb], w_o, x, tm=tiles.out_rows)
    return _ffn(x, ffn_norm, w_gate, w_up, w_down, tm=tiles.rows, tf=tiles.ff_cols)


def _sigmoid(x):
    return 0.5 + 0.5 * jnp.tanh(0.5 * x)


def _gelu_tanh(x):
    return 0.5 * x * (1.0 + jnp.tanh(math.sqrt(2.0 / math.pi) * (x + 0.044715 * (x * x * x))))


def _rglru_kernel(xb_ref, gt_ref, cw_ref, cb_ref, wa_ref, ba_ref, wi_ref, bi_ref, lam_ref, o_ref,
                  tail_ref, h_ref, *, tt):
    ti = pl.program_id(2)
    seg = tt // SUBLANES
    bw = o_ref.shape[1]

    @pl.when(ti == 0)
    def _():
        tail_ref[...] = jnp.zeros_like(tail_ref)
        h_ref[...] = jnp.zeros_like(h_ref)

    def to_slabs(v):
        return pltpu.einshape("jkl->kjl", v.reshape(SUBLANES, seg, bw))

    x3 = to_slabs(xb_ref[...].astype(F32))
    sub = lax.broadcasted_iota(jnp.int32, (SUBLANES, bw), 0)
    halo = [jnp.where(sub == 0, tail_ref[d - 1:d, :], pltpu.roll(x3[seg - d], 1, 0))
            for d in range(1, CONV_WIDTH)]
    for d in range(1, CONV_WIDTH):
        tail_ref[d - 1:d, :] = x3[seg - d][SUBLANES - 1:SUBLANES, :]
    cw = cw_ref[...]
    xc_slabs = []
    for k in range(seg):
        acc = cb_ref[...] + x3[k] * cw[CONV_WIDTH - 1:CONV_WIDTH, :]
        for d in range(1, CONV_WIDTH):
            prev = x3[k - d] if k >= d else halo[d - k - 1]
            acc = acc + prev * cw[CONV_WIDTH - 1 - d:CONV_WIDTH - d, :]
        xc_slabs.append(acc)
    xc = jnp.stack(xc_slabs, axis=0).reshape(tt, bw)

    xcb = xc.astype(BF16)
    r = _sigmoid(jnp.dot(xcb, wa_ref[0].astype(BF16), preferred_element_type=F32) + ba_ref[...])
    ig = _sigmoid(jnp.dot(xcb, wi_ref[0].astype(BF16), preferred_element_type=F32) + bi_ref[...])
    neg_lam = -lam_ref[...]
    sp = jnp.maximum(neg_lam, 0.0) + jnp.log1p(jnp.exp(-jnp.abs(neg_lam)))
    log_a = (-LRU_C * r) * sp
    a = jnp.exp(log_a)
    one_minus_a2 = -jnp.tanh(log_a) * (a * a + 1.0)
    root = jnp.where(one_minus_a2 > 0.0, one_minus_a2 * lax.rsqrt(one_minus_a2), 0.0)
    u = root * (ig * xc)

    a3 = a.reshape(seg, SUBLANES, bw)
    u3 = u.reshape(seg, SUBLANES, bw)
    h_loc = jnp.zeros((SUBLANES, bw), F32)
    prod = jnp.ones((SUBLANES, bw), F32)
    h_slabs, p_slabs = [], []
    for k in range(seg):
        h_loc = a3[k] * h_loc + u3[k]
        prod = a3[k] * prod
        h_slabs.append(h_loc)
        p_slabs.append(prod)
    carry = h_ref[0:1, :]
    carries = []
    for j in range(SUBLANES):
        carries.append(carry)
        carry = prod[j:j + 1, :] * carry + h_loc[j:j + 1, :]
    h_ref[0:1, :] = carry
    c_in = jnp.concatenate(carries, axis=0)
    h3 = jnp.stack([h_slabs[k] + p_slabs[k] * c_in for k in range(seg)], axis=0)
    y3 = h3 * _gelu_tanh(to_slabs(gt_ref[...].astype(F32)))
    o_ref[...] = pltpu.einshape("kjl->jkl", y3).reshape(tt, bw).astype(o_ref.dtype)


def _rglru(xbg, batch, seq, conv_w, conv_b, w_a, b_a, w_i, b_i, lam, *, tt):
    t = xbg.shape[0]
    width = xbg.shape[1] // 2
    bw = width // LRU_BLOCKS
    n_t = seq // tt
    kern = functools.partial(_rglru_kernel, tt=tt)
    vec = lambda a: a.reshape(1, width)
    vspec = pl.BlockSpec((1, bw), lambda b, c, i: (0, c))
    return pl.pallas_call(
        kern,
        out_shape=jax.ShapeDtypeStruct((t, width), BF16),
        grid=(batch, LRU_BLOCKS, n_t),
        in_specs=[
            pl.BlockSpec((tt, bw), lambda b, c, i: (b * n_t + i, c)),
            pl.BlockSpec((tt, bw), lambda b, c, i: (b * n_t + i, LRU_BLOCKS + c)),
            pl.BlockSpec((CONV_WIDTH, bw), lambda b, c, i: (0, c)),
            vspec,
            pl.BlockSpec((1, bw, bw), lambda b, c, i: (c, 0, 0)),
            vspec,
            pl.BlockSpec((1, bw, bw), lambda b, c, i: (c, 0, 0)),
            vspec,
            vspec,
        ],
        out_specs=pl.BlockSpec((tt, bw), lambda b, c, i: (b * n_t + i, c)),
        scratch_shapes=[
            pltpu.VMEM((SUBLANES, bw), F32),
            pltpu.VMEM((SUBLANES, bw), F32),
        ],
        compiler_params=_params(("parallel", "parallel", "arbitrary")),
        name="rglru",
    )(xbg, xbg, conv_w, vec(conv_b), w_a, vec(b_a), w_i, vec(b_i), vec(lam))


def _recurrent_block(x, seq, batch, rec_norm, w_in, conv_w, conv_b, w_a, b_a, w_i, b_i, lam, w_o, tiles):
    xbg = _norm_matmul(x, rec_norm, w_in, tm=tiles.rows, tn=tiles.proj_cols)
    y = _rglru(xbg, batch, seq, conv_w, conv_b, w_a, b_a, w_i, b_i, lam, tt=tiles.scan_rows)
    return _matmul_res([y], w_o, x, tm=tiles.out_rows)


def _router_kernel(x_ref, g_ref, wr_ref, idx_ref, wt_ref, cnt_ref, hp_ref, run_ref, *, rows):
    @pl.when(pl.program_id(0) == 0)
    def _():
        run_ref[...] = jnp.zeros_like(run_ref)

    hn = _rms(x_ref[...], g_ref[...])
    bits = lax.bitcast_convert_type(hn.astype(BF16).astype(F32), jnp.uint32)
    half = bits.shape[1] // 2
    words = jnp.bitwise_or(jnp.bitwise_and(bits[:, half:], jnp.uint32(0xFFFF0000)),
                           jnp.right_shift(bits[:, :half], jnp.uint32(16)))
    for c in range(hp_ref.shape[1]):
        hp_ref[:, c, :] = words[:, c * LANES:(c + 1) * LANES]
    wr = wr_ref[...]
    h1 = hn.astype(BF16)
    h2 = (hn - h1.astype(F32)).astype(BF16)
    w1 = wr.astype(BF16)
    w2 = (wr - w1.astype(F32)).astype(BF16)
    logits = jnp.dot(jnp.concatenate([h1, h1, h2], axis=1), jnp.concatenate([w1, w2, w1], axis=0),
                     preferred_element_type=F32)
    lane = lax.broadcasted_iota(jnp.int32, (rows, LANES), 1)
    lowest = jnp.float32(-3.0e38)
    logits = jnp.where(lane < N_EXPERTS, logits, lowest)
    m1 = jnp.max(logits, axis=1, keepdims=True)
    e1 = jnp.min(jnp.where(logits == m1, lane, LANES), axis=1, keepdims=True)
    rest = jnp.where(lane == e1, lowest, logits)
    m2 = jnp.max(rest, axis=1, keepdims=True)
    e2 = jnp.min(jnp.where(rest == m2, lane, LANES), axis=1, keepdims=True)
    ex = jnp.exp(m2 - m1)
    w1 = 1.0 / (1.0 + ex)
    w2 = ex / (1.0 + ex)

    onehot = jnp.where(jnp.logical_or(lane == e1, lane == e2), 1.0, 0.0)
    rr = lax.broadcasted_iota(jnp.int32, (rows, rows), 0)
    cc = lax.broadcasted_iota(jnp.int32, (rows, rows), 1)
    before = jnp.where(cc < rr, 1.0, 0.0).astype(BF16)
    rank = jnp.dot(before, onehot.astype(BF16), preferred_element_type=F32) + run_ref[0:1, :]
    r1 = jnp.sum(jnp.where(lane == e1, rank, 0.0), axis=1, keepdims=True).astype(jnp.int32)
    r2 = jnp.sum(jnp.where(lane == e2, rank, 0.0), axis=1, keepdims=True).astype(jnp.int32)
    run_ref[0:1, :] = run_ref[0:1, :] + jnp.sum(onehot, axis=0, keepdims=True)

    packed = jnp.where(lane == 0, e1, jnp.where(lane == 1, e2, jnp.where(lane == 2, r1, r2)))
    idx_ref[...] = packed[:, 0:idx_ref.shape[1]]
    wts = jnp.where(lane == 0, w1, w2)
    wt_ref[...] = wts[:, 0:wt_ref.shape[1]]
    cnt_ref[...] = run_ref[...].astype(jnp.int32)


def _router(x, g, router, *, rows):
    t, d = x.shape
    wr = jnp.pad(router, ((0, 0), (0, LANES - router.shape[1])))
    kern = functools.partial(_router_kernel, rows=rows)
    return pl.pallas_call(
        kern,
        out_shape=(
            jax.ShapeDtypeStruct((t, SUBLANES), jnp.int32),
            jax.ShapeDtypeStruct((t, SUBLANES), F32),
            jax.ShapeDtypeStruct((SUBLANES, LANES), jnp.int32),
            jax.ShapeDtypeStruct((t, d // 2 // LANES, LANES), jnp.uint32),
        ),
        grid=(t // rows,),
        in_specs=[
            pl.BlockSpec((rows, d), lambda i: (i, 0)),
            pl.BlockSpec((1, d), lambda i: (0, 0)),
            pl.BlockSpec((d, LANES), lambda i: (0, 0)),
        ],
        out_specs=(
            pl.BlockSpec((rows, SUBLANES), lambda i: (i, 0)),
            pl.BlockSpec((rows, SUBLANES), lambda i: (i, 0)),
            pl.BlockSpec((SUBLANES, LANES), lambda i: (0, 0)),
            pl.BlockSpec((rows, d // 2 // LANES, LANES), lambda i: (i, 0, 0)),
        ),
        scratch_shapes=[pltpu.VMEM((SUBLANES, LANES), F32)],
        compiler_params=_params(("arbitrary",)),
        name="moe_router",
    )(x, g.reshape(1, d), wr)


MOE_SUB_ROWS = 256


def _moe_ffn_kernel(te_ref, nu_ref, tv_ref, tok_ref, tokn_ref, hp_ref, wg_ref, wu_ref, wd_ref, o_ref,
                    xg_ref, hn_ref, sem, *, tm, n_f):
    del te_ref
    i = pl.program_id(0)
    f = pl.program_id(1)
    n_used = nu_ref[0]
    slot = lax.rem(i, 2)
    per_step = tm // n_f
    extra = tm - per_step * n_f

    def row_copy(tok, r, s):
        return pltpu.make_async_copy(hp_ref.at[tok[r]], xg_ref.at[s, r], sem.at[s])

    @pl.when(jnp.logical_and(i >= n_used, f == 0))
    def _():
        o_ref[...] = jnp.zeros_like(o_ref)

    @pl.when(i < n_used)
    def _():
        @pl.when(f == 0)
        def _():
            @pl.when(i == 0)
            def _():
                def start(r, carry):
                    row_copy(tok_ref, r, 0).start()
                    return carry
                lax.fori_loop(0, tm, start, 0)

            pltpu.make_async_copy(hp_ref.at[pl.ds(0, tm)], xg_ref.at[slot], sem.at[slot]).wait()
            chunks = pltpu.einshape("tcl->ctl", xg_ref[slot])
            half = hn_ref.shape[1] // 2
            for c in range(chunks.shape[0]):
                words = chunks[c]
                lo = lax.bitcast_convert_type(jnp.left_shift(words, jnp.uint32(16)), F32)
                hi = lax.bitcast_convert_type(jnp.bitwise_and(words, jnp.uint32(0xFFFF0000)), F32)
                hn_ref[:, c * LANES:(c + 1) * LANES] = lo.astype(BF16)
                hn_ref[:, half + c * LANES:half + (c + 1) * LANES] = hi.astype(BF16)
            o_ref[...] = jnp.zeros_like(o_ref)

        @pl.when(i + 1 < n_used)
        def _():
            base = f * per_step + jnp.minimum(f, extra)
            for rr in range(per_step):
                row_copy(tokn_ref, base + rr, 1 - slot).start()
            if extra:
                @pl.when(f < extra)
                def _():
                    row_copy(tokn_ref, base + per_step, 1 - slot).start()

        n_sub = lax.shift_right_logical(tv_ref[i] + (MOE_SUB_ROWS - 1), MOE_SUB_ROWS.bit_length() - 1)
        for k in range(1, tm // MOE_SUB_ROWS + 1):
            @pl.when(n_sub == k)
            def _(k=k):
                rows = k * MOE_SUB_ROWS
                hn = hn_ref[0:rows, :]
                gate = jnp.dot(hn, wg_ref[0].astype(BF16), preferred_element_type=F32)
                up = jnp.dot(hn, wu_ref[0].astype(BF16), preferred_element_type=F32)
                h = _silu_mul(gate, up).astype(BF16)
                o_ref[0:rows, :] += jnp.dot(h, wd_ref[0].astype(BF16), preferred_element_type=F32)


def _moe_ffn(hp, tok, we_gate, we_up, we_down, tile_expert, n_used, tile_valid, *, tm, tf):
    d = 2 * hp.shape[1] * hp.shape[2]
    n_rows = tok.shape[0]
    dff = we_gate.shape[2]
    n_f = dff // tf
    n_tiles = n_rows // tm
    assert tm % MOE_SUB_ROWS == 0 and MOE_SUB_ROWS & (MOE_SUB_ROWS - 1) == 0

    def tile(i, nu):
        return jnp.maximum(jnp.minimum(i, nu[0] - 1), 0)

    def fidx(i, f, nu):
        return jnp.where(i < nu[0], f, n_f - 1)

    grid_spec = pltpu.PrefetchScalarGridSpec(
        num_scalar_prefetch=3,
        grid=(n_tiles, n_f),
        in_specs=[
            pl.BlockSpec((tm,), lambda i, f, te, nu, tv: (tile(i, nu),), memory_space=pltpu.SMEM),
            pl.BlockSpec((tm,), lambda i, f, te, nu, tv: (tile(i + 1, nu),), memory_space=pltpu.SMEM),
            pl.BlockSpec(memory_space=pl.ANY),
            pl.BlockSpec((1, d, tf), lambda i, f, te, nu, tv: (te[tile(i, nu)], 0, fidx(i, f, nu))),
            pl.BlockSpec((1, d, tf), lambda i, f, te, nu, tv: (te[tile(i, nu)], 0, fidx(i, f, nu))),
            pl.BlockSpec((1, tf, d), lambda i, f, te, nu, tv: (te[tile(i, nu)], fidx(i, f, nu), 0)),
        ],
        out_specs=pl.BlockSpec((tm, d), lambda i, f, te, nu, tv: (i, 0)),
        scratch_shapes=[
            pltpu.VMEM((2, tm) + hp.shape[1:], jnp.uint32),
            pltpu.VMEM((tm, d), BF16),
            pltpu.SemaphoreType.DMA((2,)),
        ],
    )
    return pl.pallas_call(
        functools.partial(_moe_ffn_kernel, tm=tm, n_f=n_f),
        out_shape=jax.ShapeDtypeStruct((n_rows, d), F32),
        grid_spec=grid_spec,
        compiler_params=_params(("arbitrary", "arbitrary")),
        name="moe_ffn",
    )(tile_expert, n_used, tile_valid, tok, tok, hp, we_gate, we_up, we_down)


COMBINE_ROWS = 512


def _combine_kernel(pos_ref, x_ref, wt_ref, ys_ref, o_ref, y1_ref, y2_ref, sem):
    def row_copy(r, k, dst):
        return pltpu.make_async_copy(ys_ref.at[pl.ds(pos_ref[2 * r + k], 1)], dst.at[pl.ds(r, 1)], sem)

    def start(r, carry):
        row_copy(r, 0, y1_ref).start()
        row_copy(r, 1, y2_ref).start()
        return carry

    lax.fori_loop(0, COMBINE_ROWS, start, 0)
    for dst in (y1_ref, y2_ref):
        pltpu.make_async_copy(ys_ref.at[pl.ds(0, COMBINE_ROWS)], dst, sem).wait()
    wt = wt_ref[...]
    o_ref[...] = x_ref[...] + wt[:, 0:1] * y1_ref[...] + wt[:, 1:2] * y2_ref[...]


def _combine(x, wts, ys, pos_flat):
    t, d = x.shape
    return pl.pallas_call(
        _combine_kernel,
        out_shape=jax.ShapeDtypeStruct((t, d), F32),
        grid=(t // COMBINE_ROWS,),
        in_specs=[
            pl.BlockSpec((2 * COMBINE_ROWS,), lambda i: (i,), memory_space=pltpu.SMEM),
            pl.BlockSpec((COMBINE_ROWS, d), lambda i: (i, 0)),
            pl.BlockSpec((COMBINE_ROWS, SUBLANES), lambda i: (i, 0)),
            pl.BlockSpec(memory_space=pl.ANY),
        ],
        out_specs=pl.BlockSpec((COMBINE_ROWS, d), lambda i: (i, 0)),
        scratch_shapes=[
            pltpu.VMEM((COMBINE_ROWS, d), F32),
            pltpu.VMEM((COMBINE_ROWS, d), F32),
            pltpu.SemaphoreType.DMA(()),
        ],
        compiler_params=_params(("arbitrary",)),
        name="moe_combine",
    )(pos_flat, x, wts, ys)


def _moe_block(x, ffn_norm, router, we_gate, we_up, we_down, tiles):
    t, d = x.shape
    tm, tf = tiles.rows, tiles.ff_cols
    idx, wts, cnt, hp = _router(x, ffn_norm, router, rows=tiles.route_rows)
    n_tiles = (TOP_K * t) // tm + N_EXPERTS
    counts = cnt[0, :N_EXPERTS]
    padded = ((counts + tm - 1) // tm) * tm
    ends = jnp.cumsum(padded)
    offsets = ends - padded
    pos = jnp.take(offsets, idx[:, 0:2]) + idx[:, 2:4]
    pos_flat = pos.reshape(-1).astype(jnp.int32)
    tile_start = jnp.arange(n_tiles, dtype=jnp.int32) * tm
    tile_expert = jnp.minimum(jnp.sum(tile_start[:, None] >= ends[None, :], axis=1), N_EXPERTS - 1).astype(jnp.int32)
    n_used = (ends[-1:] // tm).astype(jnp.int32)
    tile_valid = jnp.clip(jnp.take(offsets + counts, tile_expert) - tile_start, 0, tm).astype(jnp.int32)
    tok = jnp.zeros((n_tiles * tm,), jnp.int32).at[pos_flat].set(jnp.repeat(jnp.arange(t, dtype=jnp.int32), TOP_K))
    ys = _moe_ffn(hp, tok, we_gate, we_up, we_down, tile_expert, n_used, tile_valid, tm=tm, tf=tf)
    return _combine(x, wts, ys, pos_flat)


def kernel(x, ev_attn_norm, ev_w_in, ev_q_norm_sb, ev_k_norm_sb, ev_q_norm_moba, ev_k_norm_moba, ev_w_o, ev_ffn_norm, ev_w_gate, ev_w_up, ev_w_down, od_rec_norm, od_w_in, od_conv_w, od_conv_b, od_w_a, od_b_a, od_w_i, od_b_i, od_lambda, od_w_o, od_ffn_norm, od_router, od_we_gate, od_we_up, od_we_down):
    batch, seq, d = x.shape
    t = batch * seq
    tiles = _tiles(seq)
    h = x.reshape(t, d)
    h = _attention_layer(h, seq, batch, ev_attn_norm[0], ev_w_in[0], ev_q_norm_sb[0], ev_k_norm_sb[0],
                         ev_q_norm_moba[0], ev_k_norm_moba[0], ev_w_o[0], ev_ffn_norm[0],
                         ev_w_gate[0], ev_w_up[0], ev_w_down[0], tiles)
    h = _recurrent_block(h, seq, batch, od_rec_norm[0], od_w_in[0], od_conv_w[0], od_conv_b[0], od_w_a[0],
                         od_b_a[0], od_w_i[0], od_b_i[0], od_lambda[0], od_w_o[0], tiles)
    h = _moe_block(h, od_ffn_norm[0], od_router[0], od_we_gate[0], od_we_up[0], od_we_down[0], tiles)
    return h.reshape(batch, seq, d)
```

```python
import functools
import math
from typing import NamedTuple

import jax
import jax.numpy as jnp
from jax import lax
from jax.experimental import pallas as pl
from jax.experimental.pallas import tpu as pltpu

F32 = jnp.float32
BF16 = jnp.bfloat16

HEAD_DIM = 128
N_HEADS_SB = 8
N_HEADS_MOBA = 8
ROPE_DIM = HEAD_DIM // 4
ROPE_THETA = 500000.0
MOBA_BLOCK = 256
MOBA_TOPK = 3
LRU_BLOCKS = 8
LRU_C = 8.0
CONV_WIDTH = 4
N_EXPERTS = 8
TOP_K = 2
NORM_EPS = 1e-6
NEG_INF = -1e30

LANES = 128
SUBLANES = 8
VMEM_LIMIT_BYTES = 56 * 1024 * 1024


class _Tiles(NamedTuple):
    rows: int
    out_rows: int
    proj_cols: int
    ff_cols: int
    sb_block: int
    scan_rows: int
    route_rows: int


def _tiles(seq):
    rows = min(1024, seq)
    return _Tiles(rows=rows, out_rows=rows // 2, proj_cols=512, ff_cols=256, sb_block=256,
                  scan_rows=min(512, seq), route_rows=256)


def _params(semantics, vmem=VMEM_LIMIT_BYTES):
    return pltpu.CompilerParams(dimension_semantics=semantics, vmem_limit_bytes=vmem)


def _rms(x, g):
    ms = jnp.mean(x * x, axis=-1, keepdims=True)
    return x * lax.rsqrt(ms + NORM_EPS) * g


def _norm_matmul_kernel(x_ref, g_ref, w_ref, o_ref, hn_ref):
    @pl.when(pl.program_id(1) == 0)
    def _():
        hn_ref[...] = _rms(x_ref[...], g_ref[...]).astype(BF16)

    y = jnp.dot(hn_ref[...], w_ref[...].astype(BF16), preferred_element_type=F32)
    o_ref[...] = y.astype(o_ref.dtype)


def _qkv_kernel(x_ref, g_ref, w_ref, hg_ref, c_ref, sa_ref, sb_ref, o_ref, hn_ref, *, tn, sec_w):
    j = pl.program_id(1)

    @pl.when(j == 0)
    def _():
        hn_ref[...] = _rms(x_ref[...], g_ref[...]).astype(BF16)

    y = jnp.dot(hn_ref[...], w_ref[...].astype(BF16), preferred_element_type=F32)
    sec = j // (sec_w // tn)
    is_v = jnp.logical_or(sec == 2, sec == 5)
    is_mb = jnp.logical_or(sec == 3, sec == 4)
    is_sb = jnp.logical_or(sec == 0, sec == 1)

    def split_dot(t, mat):
        hi = t.astype(BF16)
        lo = (t - hi.astype(F32)).astype(BF16)
        return jnp.dot(jnp.concatenate([hi, lo], axis=1), jnp.concatenate([mat, mat], axis=0),
                       preferred_element_type=F32)

    def head_norm(hh):
        t = y[:, hh * HEAD_DIM:(hh + 1) * HEAD_DIM]
        mean_mat = jnp.full((HEAD_DIM, HEAD_DIM), 1.0 / HEAD_DIM, BF16)
        ms = jnp.dot((t * t).astype(BF16), mean_mat, preferred_element_type=F32)
        return t * lax.rsqrt(ms + NORM_EPS) * hg_ref[0]

    @pl.when(is_v)
    def _():
        o_ref[...] = y.astype(o_ref.dtype)

    @pl.when(is_sb)
    def _():
        for hh in range(tn // HEAD_DIM):
            o_ref[:, hh * HEAD_DIM:(hh + 1) * HEAD_DIM] = head_norm(hh).astype(o_ref.dtype)

    @pl.when(is_mb)
    def _():
        half = ROPE_DIM // 2
        src = lax.broadcasted_iota(jnp.int32, (HEAD_DIM, 2 * HEAD_DIM), 0)
        dst = lax.broadcasted_iota(jnp.int32, (HEAD_DIM, 2 * HEAD_DIM), 1)
        from_below = jnp.logical_and(dst < HEAD_DIM, src + half == dst)
        from_above = jnp.logical_and(dst >= HEAD_DIM, src - half == dst - HEAD_DIM)
        shift_mat = jnp.where(jnp.logical_or(from_below, from_above), 1.0, 0.0).astype(BF16)
        for hh in range(tn // HEAD_DIM):
            t = head_norm(hh)
            rot = split_dot(t, shift_mat)
            t = t * c_ref[0] + rot[:, :HEAD_DIM] * sa_ref[0] + rot[:, HEAD_DIM:] * sb_ref[0]
            o_ref[:, hh * HEAD_DIM:(hh + 1) * HEAD_DIM] = t.astype(o_ref.dtype)


def _norm_matmul(x, g, w, *, tm, tn, out_dtype=BF16):
    t, d = x.shape
    n = w.shape[1]
    return pl.pallas_call(
        _norm_matmul_kernel,
        out_shape=jax.ShapeDtypeStruct((t, n), out_dtype),
        grid=(t // tm, n // tn),
        in_specs=[
            pl.BlockSpec((tm, d), lambda i, j: (i, 0)),
            pl.BlockSpec((1, d), lambda i, j: (0, 0)),
            pl.BlockSpec((d, tn), lambda i, j: (0, j)),
        ],
        out_specs=pl.BlockSpec((tm, tn), lambda i, j: (i, j)),
        scratch_shapes=[pltpu.VMEM((tm, d), BF16)],
        compiler_params=_params(("parallel", "arbitrary")),
        name="norm_matmul",
    )(x, g.reshape(1, d), w)


def _rope_tables(seq):
    half = ROPE_DIM // 2
    inv_freq = ROPE_THETA ** (-jnp.arange(0, ROPE_DIM, 2, dtype=F32) / ROPE_DIM)
    ang = jnp.arange(seq, dtype=F32)[:, None] * inv_freq[None, :]
    cos, sin = jnp.cos(ang), jnp.sin(ang)
    ones = jnp.ones((seq, HEAD_DIM - ROPE_DIM), F32)
    zeros = jnp.zeros((seq, HEAD_DIM - ROPE_DIM), F32)
    zh = jnp.zeros((seq, half), F32)
    c = jnp.concatenate([cos, cos, ones], axis=1)
    sa = jnp.concatenate([zh, sin, zeros], axis=1)
    sb = jnp.concatenate([-sin, zh, zeros], axis=1)
    ident = jnp.ones((seq, HEAD_DIM), F32)
    z = jnp.zeros((seq, HEAD_DIM), F32)
    return jnp.stack([ident, c]), jnp.stack([z, sa]), jnp.stack([z, sb])


def _qkv_proj(x, g, w, head_gains, seq, *, tm, tn):
    t, d = x.shape
    n = w.shape[1]
    sec_w = n // 6
    c, sa, sb = _rope_tables(seq)
    n_s = seq // tm
    kern = functools.partial(_qkv_kernel, tn=tn, sec_w=sec_w)
    per_sec = sec_w // tn
    return pl.pallas_call(
        kern,
        out_shape=jax.ShapeDtypeStruct((t, n), BF16),
        grid=(t // tm, n // tn),
        in_specs=[
            pl.BlockSpec((tm, d), lambda i, j: (i, 0)),
            pl.BlockSpec((1, d), lambda i, j: (0, 0)),
            pl.BlockSpec((d, tn), lambda i, j: (0, j)),
            pl.BlockSpec((1, 1, HEAD_DIM), lambda i, j: (j // per_sec, 0, 0)),
            pl.BlockSpec((1, tm, HEAD_DIM), lambda i, j: ((j // per_sec) // 3, i % n_s, 0)),
            pl.BlockSpec((1, tm, HEAD_DIM), lambda i, j: ((j // per_sec) // 3, i % n_s, 0)),
            pl.BlockSpec((1, tm, HEAD_DIM), lambda i, j: ((j // per_sec) // 3, i % n_s, 0)),
        ],
        out_specs=pl.BlockSpec((tm, tn), lambda i, j: (i, j)),
        scratch_shapes=[pltpu.VMEM((tm, d), BF16)],
        compiler_params=_params(("parallel", "arbitrary")),
        name="qkv_proj",
    )(x, g.reshape(1, d), w, head_gains, c, sa, sb)


def _matmul_res_kernel(*refs, n_a):
    a_refs = refs[:n_a]
    w_ref, r_ref, o_ref, wb_ref = refs[n_a:]

    @pl.when(pl.program_id(0) == 0)
    def _():
        wb_ref[...] = w_ref[...].astype(BF16)

    acc = r_ref[...]
    k0 = 0
    for a_ref in a_refs:
        k = a_ref.shape[1]
        acc = acc + jnp.dot(a_ref[...], wb_ref[k0:k0 + k, :], preferred_element_type=F32)
        k0 += k
    o_ref[...] = acc


def _matmul_res(a_list, w, res, *, tm):
    t = res.shape[0]
    k, n = w.shape
    kern = functools.partial(_matmul_res_kernel, n_a=len(a_list))
    a_specs = [pl.BlockSpec((tm, a.shape[1]), lambda i: (i, 0)) for a in a_list]
    return pl.pallas_call(
        kern,
        out_shape=jax.ShapeDtypeStruct((t, n), F32),
        grid=(t // tm,),
        in_specs=a_specs + [
            pl.BlockSpec((k, n), lambda i: (0, 0), pipeline_mode=pl.Buffered(1)),
            pl.BlockSpec((tm, n), lambda i: (i, 0)),
        ],
        out_specs=pl.BlockSpec((tm, n), lambda i: (i, 0)),
        scratch_shapes=[pltpu.VMEM((k, n), BF16)],
        compiler_params=_params(("arbitrary",)),
        name="matmul_res",
    )(*a_list, w, res)


def _silu_mul(gate, up):
    return gate * (1.0 / (1.0 + jnp.exp(-gate))) * up


def _ffn_kernel(x_ref, g_ref, wg_ref, wu_ref, wd_ref, o_ref, hn_ref):
    @pl.when(pl.program_id(1) == 0)
    def _():
        x = x_ref[...]
        hn_ref[...] = _rms(x, g_ref[...]).astype(BF16)
        o_ref[...] = x

    hn = hn_ref[...]
    gate = jnp.dot(hn, wg_ref[...].astype(BF16), preferred_element_type=F32)
    up = jnp.dot(hn, wu_ref[...].astype(BF16), preferred_element_type=F32)
    h = _silu_mul(gate, up).astype(BF16)
    o_ref[...] += jnp.dot(h, wd_ref[...].astype(BF16), preferred_element_type=F32)


def _ffn(x, g, wg, wu, wd, *, tm, tf):
    t, d = x.shape
    dff = wg.shape[1]
    return pl.pallas_call(
        _ffn_kernel,
        out_shape=jax.ShapeDtypeStruct((t, d), F32),
        grid=(t // tm, dff // tf),
        in_specs=[
            pl.BlockSpec((tm, d), lambda i, f: (i, 0), pipeline_mode=pl.Buffered(1)),
            pl.BlockSpec((1, d), lambda i, f: (0, 0)),
            pl.BlockSpec((d, tf), lambda i, f: (0, f)),
            pl.BlockSpec((d, tf), lambda i, f: (0, f)),
            pl.BlockSpec((tf, d), lambda i, f: (f, 0)),
        ],
        out_specs=pl.BlockSpec((tm, d), lambda i, f: (i, 0)),
        scratch_shapes=[pltpu.VMEM((tm, d), BF16)],
        compiler_params=_params(("parallel", "arbitrary")),
        name="ffn_swiglu",
    )(x, g.reshape(1, d), wg, wu, wd)


def _transpose_v(v_ref, vt_ref, blk):
    for c in range(v_ref.shape[0] // blk):
        vt_ref[:, c * blk:(c + 1) * blk] = v_ref[c * blk:(c + 1) * blk, :].astype(F32).T.astype(vt_ref.dtype)


SB_DEAD_LOG_WEIGHT = -110.0


def _sb_kernel(q_ref, k_ref, v_ref, o_ref, vt_ref, acc_ref, *, blk):
    qi = pl.program_id(2)

    @pl.when(qi == 0)
    def _():
        _transpose_v(v_ref, vt_ref, blk)

    q = q_ref[...]
    key = lax.broadcasted_iota(jnp.int32, (blk, blk), 0)
    qry = lax.broadcasted_iota(jnp.int32, (blk, blk), 1)
    later_mat = jnp.where(qry > key, 1.0, 0.0).astype(BF16)
    later_mat2 = jnp.concatenate([later_mat, later_mat], axis=1)

    def log_weights(kj, diagonal):
        start = pl.multiple_of(kj * blk, blk)
        z = lax.dot_general(k_ref[pl.ds(start, blk), :], q, (((1,), (1,)), ((), ())),
                            preferred_element_type=F32)
        sp = jnp.log(1.0 + jnp.exp(-jnp.abs(z)))
        log_beta = jnp.minimum(z, 0.0) - sp
        log_keep = log_beta - z
        if diagonal:
            past = key < qry
            log_keep = jnp.where(past, log_keep, 0.0)
        lk_hi = log_keep.astype(BF16)
        lk_lo = (log_keep - lk_hi.astype(F32)).astype(BF16)
        later = jnp.dot(later_mat2, jnp.concatenate([lk_hi, lk_lo], axis=0), preferred_element_type=F32)
        lw = log_beta + later
        if diagonal:
            lw = jnp.where(past, lw, NEG_INF)
        return start, lw, jnp.sum(log_keep, axis=0, keepdims=True)

    def weighted_values(start, lw, c):
        w = jnp.exp(lw + c).astype(BF16)
        return jnp.dot(vt_ref[:, pl.ds(start, blk)], w, preferred_element_type=F32)

    has_prev = qi > 0
    st0, lw0, sum0 = log_weights(qi, True)
    st1, lw1, sum1 = log_weights(jnp.maximum(qi - 1, 0), False)
    acc_ref[...] = (weighted_values(st0, lw0, jnp.zeros((1, blk), F32))
                    + weighted_values(st1, lw1, jnp.where(has_prev, sum0, NEG_INF)))
    c = sum0 + jnp.where(has_prev, sum1, 0.0)

    def cond(carry):
        kj, _, alive = carry
        return jnp.logical_and(kj >= 0, alive)

    def body(carry):
        kj, c, _ = carry
        st, lw, sm = log_weights(kj, False)
        acc_ref[...] += weighted_values(st, lw, c)
        c = c + sm
        return kj - 1, c, jnp.max(c) > SB_DEAD_LOG_WEIGHT

    lax.while_loop(cond, body, (qi - 2, c, jnp.max(c) > SB_DEAD_LOG_WEIGHT))
    o_ref[...] = acc_ref[...].T.astype(o_ref.dtype)


def _sb_attention(qkv, batch, seq, *, blk):
    t = qkv.shape[0]
    bq = blk
    nq = seq // bq
    kern = functools.partial(_sb_kernel, blk=blk)
    return pl.pallas_call(
        kern,
        out_shape=jax.ShapeDtypeStruct((t, N_HEADS_SB * HEAD_DIM), BF16),
        grid=(batch, N_HEADS_SB, nq),
        in_specs=[
            pl.BlockSpec((bq, HEAD_DIM), lambda b, h, i: (b * nq + i, h)),
            pl.BlockSpec((seq, HEAD_DIM), lambda b, h, i: (b, N_HEADS_SB + h)),
            pl.BlockSpec((seq, HEAD_DIM), lambda b, h, i: (b, 2 * N_HEADS_SB + h)),
        ],
        out_specs=pl.BlockSpec((bq, HEAD_DIM), lambda b, h, i: (b * nq + i, h)),
        scratch_shapes=[pltpu.VMEM((HEAD_DIM, seq), BF16), pltpu.VMEM((HEAD_DIM, bq), F32)],
        compiler_params=_params(("parallel", "parallel", "arbitrary")),
        name="sb_attention",
    )(qkv, qkv, qkv)


MOBA_GROUP = 2


def _moba_kernel(q_ref, k_ref, v_ref, o_ref, vt_ref, kmean_ref, *, blk, seq):
    qi = pl.program_id(2)
    n_blocks = seq // blk

    @pl.when(qi == 0)
    def _():
        _transpose_v(v_ref, vt_ref, blk)
        r = lax.broadcasted_iota(jnp.int32, (n_blocks, seq), 0)
        s = lax.broadcasted_iota(jnp.int32, (n_blocks, seq), 1)
        avg = jnp.where(s // blk == r, 1.0 / blk, 0.0).astype(BF16)
        kmean_ref[...] = jnp.dot(avg, k_ref[...], preferred_element_type=F32)

    q = q_ref[...]
    kmean = kmean_ref[...]
    km1 = kmean.astype(BF16)
    rest = kmean - km1.astype(F32)
    km2 = rest.astype(BF16)
    km3 = (rest - km2.astype(F32)).astype(BF16)
    gate = lax.dot_general(jnp.concatenate([km1, km2, km3], axis=1), jnp.concatenate([q, q, q], axis=1),
                           (((1,), (1,)), ((), ())), preferred_element_type=F32)
    blk_id = lax.broadcasted_iota(jnp.int32, (n_blocks, blk), 0)
    lowest = jnp.float32(-3.0e38)
    is_past = blk_id < qi
    g = jnp.where(is_past, gate, lowest)
    bias = jnp.full((n_blocks, blk), NEG_INF, F32)
    for _ in range(MOBA_TOPK):
        mx = jnp.max(g, axis=0, keepdims=True)
        first = jnp.min(jnp.where(g == mx, blk_id, n_blocks), axis=0, keepdims=True)
        pick = blk_id == first
        bias = jnp.where(jnp.logical_and(pick, is_past), 0.0, bias)
        g = jnp.where(pick, lowest, g)
    bias = jnp.where(blk_id == qi, 0.0, bias)

    key = lax.broadcasted_iota(jnp.int32, (blk, blk), 0)
    qry = lax.broadcasted_iota(jnp.int32, (blk, blk), 1) + qi * blk

    def attend(extent):
        s = lax.dot_general(k_ref[0:extent * blk, :], q, (((1,), (1,)), ((), ())),
                            preferred_element_type=F32)
        parts = []
        for n in range(extent):
            part = s[n * blk:(n + 1) * blk, :] + bias[n:n + 1, :]
            if n >= extent - MOBA_GROUP:
                part = jnp.where(key + n * blk <= qry, part, NEG_INF)
            parts.append(part)
        s = jnp.concatenate(parts, axis=0)
        m = jnp.max(s, axis=0, keepdims=True)
        p = jnp.exp2(s - m)
        l = jnp.sum(p, axis=0, keepdims=True)
        acc = jnp.dot(vt_ref[:, 0:extent * blk], p.astype(BF16), preferred_element_type=F32)
        o_ref[...] = (acc / l).T.astype(o_ref.dtype)

    n_groups = -(-n_blocks // MOBA_GROUP)
    for grp in range(n_groups):
        @pl.when(qi // MOBA_GROUP == grp)
        def _(grp=grp):
            attend(min((grp + 1) * MOBA_GROUP, n_blocks))


def _moba_attention(qkv, batch, seq):
    t = qkv.shape[0]
    blk = MOBA_BLOCK
    nq = seq // blk
    col0 = 3 * N_HEADS_SB
    kern = functools.partial(_moba_kernel, blk=blk, seq=seq)
    return pl.pallas_call(
        kern,
        out_shape=jax.ShapeDtypeStruct((t, N_HEADS_MOBA * HEAD_DIM), BF16),
        grid=(batch, N_HEADS_MOBA, nq),
        in_specs=[
            pl.BlockSpec((blk, HEAD_DIM), lambda b, h, i: (b * nq + i, col0 + h)),
            pl.BlockSpec((seq, HEAD_DIM), lambda b, h, i: (b, col0 + N_HEADS_MOBA + h)),
            pl.BlockSpec((seq, HEAD_DIM), lambda b, h, i: (b, col0 + 2 * N_HEADS_MOBA + h)),
        ],
        out_specs=pl.BlockSpec((blk, HEAD_DIM), lambda b, h, i: (b * nq + i, h)),
        scratch_shapes=[
            pltpu.VMEM((HEAD_DIM, seq), BF16),
            pltpu.VMEM((seq // blk, HEAD_DIM), F32),
        ],
        compiler_params=_params(("parallel", "parallel", "arbitrary")),
        name="moba_attention",
    )(qkv, qkv, qkv)


def _attention_layer(x, seq, batch, attn_norm, w_in, qn_sb, kn_sb, qn_mb, kn_mb, w_o,
                     ffn_norm, w_gate, w_up, w_down, tiles):
    scale = HEAD_DIM ** -0.5
    ones = jnp.ones((HEAD_DIM,), F32)
    head_gains = jnp.stack([qn_sb * scale, kn_sb, ones, qn_mb * (scale * math.log2(math.e)), kn_mb, ones])
    head_gains = head_gains.reshape(6, 1, HEAD_DIM)
    qkv = _qkv_proj(x, attn_norm, w_in, head_gains, seq, tm=tiles.rows, tn=tiles.proj_cols)
    o_sb = _sb_attention(qkv, batch, seq, blk=tiles.sb_block)
    o_mb = _moba_attention(qkv, batch, seq)
    x = _matmul_res([o_sb, o_mb], w_o, x, tm=tiles.out_rows)
    return _ffn(x, ffn_norm, w_gate, w_up, w_down, tm=tiles.rows, tf=tiles.ff_cols)


def _sigmoid(x):
    return 0.5 + 0.5 * jnp.tanh(0.5 * x)


def _gelu_tanh(x):
    return 0.5 * x * (1.0 + jnp.tanh(math.sqrt(2.0 / math.pi) * (x + 0.044715 * (x * x * x))))


def _rglru_kernel(xb_ref, gt_ref, cw_ref, cb_ref, wa_ref, ba_ref, wi_ref, bi_ref, lam_ref, o_ref,
                  tail_ref, h_ref, *, tt):
    ti = pl.program_id(2)
    seg = tt // SUBLANES
    bw = o_ref.shape[1]

    @pl.when(ti == 0)
    def _():
        tail_ref[...] = jnp.zeros_like(tail_ref)
        h_ref[...] = jnp.zeros_like(h_ref)

    def to_slabs(v):
        return pltpu.einshape("jkl->kjl", v.reshape(SUBLANES, seg, bw))

    x3 = to_slabs(xb_ref[...].astype(F32))
    sub = lax.broadcasted_iota(jnp.int32, (SUBLANES, bw), 0)
    halo = [jnp.where(sub == 0, tail_ref[d - 1:d, :], pltpu.roll(x3[seg - d], 1, 0))
            for d in range(1, CONV_WIDTH)]
    for d in range(1, CONV_WIDTH):
        tail_ref[d - 1:d, :] = x3[seg - d][SUBLANES - 1:SUBLANES, :]
    cw = cw_ref[...]
    xc_slabs = []
    for k in range(seg):
        acc = cb_ref[...] + x3[k] * cw[CONV_WIDTH - 1:CONV_WIDTH, :]
        for d in range(1, CONV_WIDTH):
            prev = x3[k - d] if k >= d else halo[d - k - 1]
            acc = acc + prev * cw[CONV_WIDTH - 1 - d:CONV_WIDTH - d, :]
        xc_slabs.append(acc)
    xc = jnp.stack(xc_slabs, axis=0).reshape(tt, bw)

    xcb = xc.astype(BF16)
    r = _sigmoid(jnp.dot(xcb, wa_ref[0].astype(BF16), preferred_element_type=F32) + ba_ref[...])
    ig = _sigmoid(jnp.dot(xcb, wi_ref[0].astype(BF16), preferred_element_type=F32) + bi_ref[...])
    neg_lam = -lam_ref[...]
    sp = jnp.maximum(neg_lam, 0.0) + jnp.log1p(jnp.exp(-jnp.abs(neg_lam)))
    log_a = (-LRU_C * r) * sp
    a = jnp.exp(log_a)
    one_minus_a2 = -jnp.tanh(log_a) * (a * a + 1.0)
    root = jnp.where(one_minus_a2 > 0.0, one_minus_a2 * lax.rsqrt(one_minus_a2), 0.0)
    u = root * (ig * xc)

    a3 = a.reshape(seg, SUBLANES, bw)
    u3 = u.reshape(seg, SUBLANES, bw)
    h_loc = jnp.zeros((SUBLANES, bw), F32)
    prod = jnp.ones((SUBLANES, bw), F32)
    h_slabs, p_slabs = [], []
    for k in range(seg):
        h_loc = a3[k] * h_loc + u3[k]
        prod = a3[k] * prod
        h_slabs.append(h_loc)
        p_slabs.append(prod)
    carry = h_ref[0:1, :]
    carries = []
    for j in range(SUBLANES):
        carries.append(carry)
        carry = prod[j:j + 1, :] * carry + h_loc[j:j + 1, :]
    h_ref[0:1, :] = carry
    c_in = jnp.concatenate(carries, axis=0)
    h3 = jnp.stack([h_slabs[k] + p_slabs[k] * c_in for k in range(seg)], axis=0)
    y3 = h3 * _gelu_tanh(to_slabs(gt_ref[...].astype(F32)))
    o_ref[...] = pltpu.einshape("kjl->jkl", y3).reshape(tt, bw).astype(o_ref.dtype)


def _rglru(xbg, batch, seq, conv_w, conv_b, w_a, b_a, w_i, b_i, lam, *, tt):
    t = xbg.shape[0]
    width = xbg.shape[1] // 2
    bw = width // LRU_BLOCKS
    n_t = seq // tt
    kern = functools.partial(_rglru_kernel, tt=tt)
    vec = lambda a: a.reshape(1, width)
    vspec = pl.BlockSpec((1, bw), lambda b, c, i: (0, c))
    return pl.pallas_call(
        kern,
        out_shape=jax.ShapeDtypeStruct((t, width), BF16),
        grid=(batch, LRU_BLOCKS, n_t),
        in_specs=[
            pl.BlockSpec((tt, bw), lambda b, c, i: (b * n_t + i, c)),
            pl.BlockSpec((tt, bw), lambda b, c, i: (b * n_t + i, LRU_BLOCKS + c)),
            pl.BlockSpec((CONV_WIDTH, bw), lambda b, c, i: (0, c)),
            vspec,
            pl.BlockSpec((1, bw, bw), lambda b, c, i: (c, 0, 0)),
            vspec,
            pl.BlockSpec((1, bw, bw), lambda b, c, i: (c, 0, 0)),
            vspec,
            vspec,
        ],
        out_specs=pl.BlockSpec((tt, bw), lambda b, c, i: (b * n_t + i, c)),
        scratch_shapes=[
            pltpu.VMEM((SUBLANES, bw), F32),
            pltpu.VMEM((SUBLANES, bw), F32),
        ],
        compiler_params=_params(("parallel", "parallel", "arbitrary")),
        name="rglru",
    )(xbg, xbg, conv_w, vec(conv_b), w_a, vec(b_a), w_i, vec(b_i), vec(lam))


def _recurrent_block(x, seq, batch, rec_norm, w_in, conv_w, conv_b, w_a, b_a, w_i, b_i, lam, w_o, tiles):
    xbg = _norm_matmul(x, rec_norm, w_in, tm=tiles.rows, tn=tiles.proj_cols)
    y = _rglru(xbg, batch, seq, conv_w, conv_b, w_a, b_a, w_i, b_i, lam, tt=tiles.scan_rows)
    return _matmul_res([y], w_o, x, tm=tiles.out_rows)


def _router_kernel(x_ref, g_ref, wr_ref, idx_ref, wt_ref, cnt_ref, hp_ref, run_ref, *, rows):
    @pl.when(pl.program_id(0) == 0)
    def _():
        run_ref[...] = jnp.zeros_like(run_ref)

    hn = _rms(x_ref[...], g_ref[...])
    bits = lax.bitcast_convert_type(hn.astype(BF16).astype(F32), jnp.uint32)
    half = bits.shape[1] // 2
    words = jnp.bitwise_or(jnp.bitwise_and(bits[:, half:], jnp.uint32(0xFFFF0000)),
                           jnp.right_shift(bits[:, :half], jnp.uint32(16)))
    for c in range(hp_ref.shape[1]):
        hp_ref[:, c, :] = words[:, c * LANES:(c + 1) * LANES]
    wr = wr_ref[...]
    h1 = hn.astype(BF16)
    h2 = (hn - h1.astype(F32)).astype(BF16)
    w1 = wr.astype(BF16)
    w2 = (wr - w1.astype(F32)).astype(BF16)
    logits = jnp.dot(jnp.concatenate([h1, h1, h2], axis=1), jnp.concatenate([w1, w2, w1], axis=0),
                     preferred_element_type=F32)
    lane = lax.broadcasted_iota(jnp.int32, (rows, LANES), 1)
    lowest = jnp.float32(-3.0e38)
    logits = jnp.where(lane < N_EXPERTS, logits, lowest)
    m1 = jnp.max(logits, axis=1, keepdims=True)
    e1 = jnp.min(jnp.where(logits == m1, lane, LANES), axis=1, keepdims=True)
    rest = jnp.where(lane == e1, lowest, logits)
    m2 = jnp.max(rest, axis=1, keepdims=True)
    e2 = jnp.min(jnp.where(rest == m2, lane, LANES), axis=1, keepdims=True)
    ex = jnp.exp(m2 - m1)
    w1 = 1.0 / (1.0 + ex)
    w2 = ex / (1.0 + ex)

    onehot = jnp.where(jnp.logical_or(lane == e1, lane == e2), 1.0, 0.0)
    rr = lax.broadcasted_iota(jnp.int32, (rows, rows), 0)
    cc = lax.broadcasted_iota(jnp.int32, (rows, rows), 1)
    before = jnp.where(cc < rr, 1.0, 0.0).astype(BF16)
    rank = jnp.dot(before, onehot.astype(BF16), preferred_element_type=F32) + run_ref[0:1, :]
    r1 = jnp.sum(jnp.where(lane == e1, rank, 0.0), axis=1, keepdims=True).astype(jnp.int32)
    r2 = jnp.sum(jnp.where(lane == e2, rank, 0.0), axis=1, keepdims=True).astype(jnp.int32)
    run_ref[0:1, :] = run_ref[0:1, :] + jnp.sum(onehot, axis=0, keepdims=True)

    packed = jnp.where(lane == 0, e1, jnp.where(lane == 1, e2, jnp.where(lane == 2, r1, r2)))
    idx_ref[...] = packed[:, 0:idx_ref.shape[1]]
    wts = jnp.where(lane == 0, w1, w2)
    wt_ref[...] = wts[:, 0:wt_ref.shape[1]]
    cnt_ref[...] = run_ref[...].astype(jnp.int32)


def _router(x, g, router, *, rows):
    t, d = x.shape
    wr = jnp.pad(router, ((0, 0), (0, LANES - router.shape[1])))
    kern = functools.partial(_router_kernel, rows=rows)
    return pl.pallas_call(
        kern,
        out_shape=(
            jax.ShapeDtypeStruct((t, SUBLANES), jnp.int32),
            jax.ShapeDtypeStruct((t, SUBLANES), F32),
            jax.ShapeDtypeStruct((SUBLANES, LANES), jnp.int32),
            jax.ShapeDtypeStruct((t, d // 2 // LANES, LANES), jnp.uint32),
        ),
        grid=(t // rows,),
        in_specs=[
            pl.BlockSpec((rows, d), lambda i: (i, 0)),
            pl.BlockSpec((1, d), lambda i: (0, 0)),
            pl.BlockSpec((d, LANES), lambda i: (0, 0)),
        ],
        out_specs=(
            pl.BlockSpec((rows, SUBLANES), lambda i: (i, 0)),
            pl.BlockSpec((rows, SUBLANES), lambda i: (i, 0)),
            pl.BlockSpec((SUBLANES, LANES), lambda i: (0, 0)),
            pl.BlockSpec((rows, d // 2 // LANES, LANES), lambda i: (i, 0, 0)),
        ),
        scratch_shapes=[pltpu.VMEM((SUBLANES, LANES), F32)],
        compiler_params=_params(("arbitrary",)),
        name="moe_router",
    )(x, g.reshape(1, d), wr)


MOE_SUB_ROWS = 256


def _moe_ffn_kernel(te_ref, nu_ref, tv_ref, tok_ref, tokn_ref, hp_ref, wg_ref, wu_ref, wd_ref, o_ref,
                    xg_ref, hn_ref, sem, *, tm, n_f):
    del te_ref
    i = pl.program_id(0)
    f = pl.program_id(1)
    n_used = nu_ref[0]
    slot = lax.rem(i, 2)
    per_step = tm // n_f
    extra = tm - per_step * n_f

    def row_copy(tok, r, s):
        return pltpu.make_async_copy(hp_ref.at[tok[r]], xg_ref.at[s, r], sem.at[s])

    @pl.when(jnp.logical_and(i >= n_used, f == 0))
    def _():
        o_ref[...] = jnp.zeros_like(o_ref)

    @pl.when(i < n_used)
    def _():
        @pl.when(f == 0)
        def _():
            @pl.when(i == 0)
            def _():
                def start(r, carry):
                    row_copy(tok_ref, r, 0).start()
                    return carry
                lax.fori_loop(0, tm, start, 0)

            pltpu.make_async_copy(hp_ref.at[pl.ds(0, tm)], xg_ref.at[slot], sem.at[slot]).wait()
            chunks = pltpu.einshape("tcl->ctl", xg_ref[slot])
            half = hn_ref.shape[1] // 2
            for c in range(chunks.shape[0]):
                words = chunks[c]
                lo = lax.bitcast_convert_type(jnp.left_shift(words, jnp.uint32(16)), F32)
                hi = lax.bitcast_convert_type(jnp.bitwise_and(words, jnp.uint32(0xFFFF0000)), F32)
                hn_ref[:, c * LANES:(c + 1) * LANES] = lo.astype(BF16)
                hn_ref[:, half + c * LANES:half + (c + 1) * LANES] = hi.astype(BF16)
            o_ref[...] = jnp.zeros_like(o_ref)

        @pl.when(i + 1 < n_used)
        def _():
            base = f * per_step + jnp.minimum(f, extra)
            for rr in range(per_step):
                row_copy(tokn_ref, base + rr, 1 - slot).start()
            if extra:
                @pl.when(f < extra)
                def _():
                    row_copy(tokn_ref, base + per_step, 1 - slot).start()

        n_sub = lax.shift_right_logical(tv_ref[i] + (MOE_SUB_ROWS - 1), MOE_SUB_ROWS.bit_length() - 1)
        for k in range(1, tm // MOE_SUB_ROWS + 1):
            @pl.when(n_sub == k)
            def _(k=k):
                rows = k * MOE_SUB_ROWS
                hn = hn_ref[0:rows, :]
                gate = jnp.dot(hn, wg_ref[0].astype(BF16), preferred_element_type=F32)
                up = jnp.dot(hn, wu_ref[0].astype(BF16), preferred_element_type=F32)
                h = _silu_mul(gate, up).astype(BF16)
                o_ref[0:rows, :] += jnp.dot(h, wd_ref[0].astype(BF16), preferred_element_type=F32)


def _moe_ffn(hp, tok, we_gate, we_up, we_down, tile_expert, n_used, tile_valid, *, tm, tf):
    d = 2 * hp.shape[1] * hp.shape[2]
    n_rows = tok.shape[0]
    dff = we_gate.shape[2]
    n_f = dff // tf
    n_tiles = n_rows // tm
    assert tm % MOE_SUB_ROWS == 0 and MOE_SUB_ROWS & (MOE_SUB_ROWS - 1) == 0

    def tile(i, nu):
        return jnp.maximum(jnp.minimum(i, nu[0] - 1), 0)

    def fidx(i, f, nu):
        return jnp.where(i < nu[0], f, n_f - 1)

    grid_spec = pltpu.PrefetchScalarGridSpec(
        num_scalar_prefetch=3,
        grid=(n_tiles, n_f),
        in_specs=[
            pl.BlockSpec((tm,), lambda i, f, te, nu, tv: (tile(i, nu),), memory_space=pltpu.SMEM),
            pl.BlockSpec((tm,), lambda i, f, te, nu, tv: (tile(i + 1, nu),), memory_space=pltpu.SMEM),
            pl.BlockSpec(memory_space=pl.ANY),
            pl.BlockSpec((1, d, tf), lambda i, f, te, nu, tv: (te[tile(i, nu)], 0, fidx(i, f, nu))),
            pl.BlockSpec((1, d, tf), lambda i, f, te, nu, tv: (te[tile(i, nu)], 0, fidx(i, f, nu))),
            pl.BlockSpec((1, tf, d), lambda i, f, te, nu, tv: (te[tile(i, nu)], fidx(i, f, nu), 0)),
        ],
        out_specs=pl.BlockSpec((tm, d), lambda i, f, te, nu, tv: (i, 0)),
        scratch_shapes=[
            pltpu.VMEM((2, tm) + hp.shape[1:], jnp.uint32),
            pltpu.VMEM((tm, d), BF16),
            pltpu.SemaphoreType.DMA((2,)),
        ],
    )
    return pl.pallas_call(
        functools.partial(_moe_ffn_kernel, tm=tm, n_f=n_f),
        out_shape=jax.ShapeDtypeStruct((n_rows, d), F32),
        grid_spec=grid_spec,
        compiler_params=_params(("arbitrary", "arbitrary")),
        name="moe_ffn",
    )(tile_expert, n_used, tile_valid, tok, tok, hp, we_gate, we_up, we_down)


COMBINE_ROWS = 512


def _combine_kernel(pos_ref, x_ref, wt_ref, ys_ref, o_ref, y1_ref, y2_ref, sem):
    def row_copy(r, k, dst):
        return pltpu.make_async_copy(ys_ref.at[pl.ds(pos_ref[2 * r + k], 1)], dst.at[pl.ds(r, 1)], sem)

    def start(r, carry):
        row_copy(r, 0, y1_ref).start()
        row_copy(r, 1, y2_ref).start()
        return carry

    lax.fori_loop(0, COMBINE_ROWS, start, 0)
    for dst in (y1_ref, y2_ref):
        pltpu.make_async_copy(ys_ref.at[pl.ds(0, COMBINE_ROWS)], dst, sem).wait()
    wt = wt_ref[...]
    o_ref[...] = x_ref[...] + wt[:, 0:1] * y1_ref[...] + wt[:, 1:2] * y2_ref[...]


def _combine(x, wts, ys, pos_flat):
    t, d = x.shape
    return pl.pallas_call(
        _combine_kernel,
        out_shape=jax.ShapeDtypeStruct((t, d), F32),
        grid=(t // COMBINE_ROWS,),
        in_specs=[
            pl.BlockSpec((2 * COMBINE_ROWS,), lambda i: (i,), memory_space=pltpu.SMEM),
            pl.BlockSpec((COMBINE_ROWS, d), lambda i: (i, 0)),
            pl.BlockSpec((COMBINE_ROWS, SUBLANES), lambda i: (i, 0)),
            pl.BlockSpec(memory_space=pl.ANY),
        ],
        out_specs=pl.BlockSpec((COMBINE_ROWS, d), lambda i: (i, 0)),
        scratch_shapes=[
            pltpu.VMEM((COMBINE_ROWS, d), F32),
            pltpu.VMEM((COMBINE_ROWS, d), F32),
            pltpu.SemaphoreType.DMA(()),
        ],
        compiler_params=_params(("arbitrary",)),
        name="moe_combine",
    )(pos_flat, x, wts, ys)


def _moe_block(x, ffn_norm, router, we_gate, we_up, we_down, tiles):
    t, d = x.shape
    tm, tf = tiles.rows, tiles.ff_cols
    idx, wts, cnt, hp = _router(x, ffn_norm, router, rows=tiles.route_rows)
    n_tiles = (TOP_K * t) // tm + N_EXPERTS
    counts = cnt[0, :N_EXPERTS]
    padded = ((counts + tm - 1) // tm) * tm
    ends = jnp.cumsum(padded)
    offsets = ends - padded
    pos = jnp.take(offsets, idx[:, 0:2]) + idx[:, 2:4]
    pos_flat = pos.reshape(-1).astype(jnp.int32)
    tile_start = jnp.arange(n_tiles, dtype=jnp.int32) * tm
    tile_expert = jnp.minimum(jnp.sum(tile_start[:, None] >= ends[None, :], axis=1), N_EXPERTS - 1).astype(jnp.int32)
    n_used = (ends[-1:] // tm).astype(jnp.int32)
    tile_valid = jnp.clip(jnp.take(offsets + counts, tile_expert) - tile_start, 0, tm).astype(jnp.int32)
    tok = jnp.zeros((n_tiles * tm,), jnp.int32).at[pos_flat].set(jnp.repeat(jnp.arange(t, dtype=jnp.int32), TOP_K))
    ys = _moe_ffn(hp, tok, we_gate, we_up, we_down, tile_expert, n_used, tile_valid, tm=tm, tf=tf)
    return _combine(x, wts, ys, pos_flat)


def kernel(x, ev_attn_norm, ev_w_in, ev_q_norm_sb, ev_k_norm_sb, ev_q_norm_moba, ev_k_norm_moba, ev_w_o, ev_ffn_norm, ev_w_gate, ev_w_up, ev_w_down, od_rec_norm, od_w_in, od_conv_w, od_conv_b, od_w_a, od_b_a, od_w_i, od_b_i, od_lambda, od_w_o, od_ffn_norm, od_router, od_we_gate, od_we_up, od_we_down):
    batch, seq, d = x.shape
    t = batch * seq
    tiles = _tiles(seq)
    h = x.reshape(t, d)
    h = _attention_layer(h, seq, batch, ev_attn_norm[0], ev_w_in[0], ev_q_norm_sb[0], ev_k_norm_sb[0],
                         ev_q_norm_moba[0], ev_k_norm_moba[0], ev_w_o[0], ev_ffn_norm[0],
                         ev_w_gate[0], ev_w_up[0], ev_w_down[0], tiles)
    h = _recurrent_block(h, seq, batch, od_rec_norm[0], od_w_in[0], od_conv_w[0], od_conv_b[0], od_w_a[0],
                         od_b_a[0], od_w_i[0], od_b_i[0], od_lambda[0], od_w_o[0], tiles)
    h = _moe_block(h, od_ffn_norm[0], od_router[0], od_we_gate[0], od_we_up[0], od_we_down[0], tiles)
    return h.reshape(batch, seq, d)
```
